```python
import math
import jax, jax.numpy as jnp
from jax import lax
import numpy as np

D_MODEL = 1024
BATCH = 8
SEQ = 2048
DEPTH = 4

CTX_LEN = 256
GRID_W = 64
MLA_HEADS = 8
MLA_NOPE = 64
MLA_ROPE = 32
MLA_V = 64
MLA_QK = MLA_NOPE + MLA_ROPE
Q_LORA = 768
KV_LORA = 256
NA_HEADS = 8
NA_DIM = 64
NA_KH = 8
NA_KW = 16
D_FF = 4 * D_MODEL
N_BRANCH = 2
ROPE_BASE = 10000.0
ROPE_PAIRS = MLA_ROPE // 4
EPS = 1e-6
Q_BLOCK = 128
OFF_CQ = N_BRANCH * D_MODEL
OFF_CKV = OFF_CQ + Q_LORA
OFF_KR = OFF_CKV + KV_LORA
OFF_NA = OFF_KR + MLA_ROPE
IN_COLS = OFF_NA + 3 * NA_HEADS * NA_DIM

kernel_name = "hybrid_mla_natten_dit_trunk"


def rms_norm(x, g):
    xf = x.astype(jnp.float32)
    y = xf * lax.rsqrt(jnp.mean(xf * xf, axis=-1, keepdims=True) + EPS)
    return (y * g.astype(jnp.float32)).astype(x.dtype)


def modulate(h, shift, scale):
    return h * (1 + scale) + shift


def rope_half(x, ang):
    m = x.shape[-1] // 2
    x1, x2 = x[..., :m], x[..., m:]
    cos = jnp.cos(ang).astype(x.dtype)
    sin = jnp.sin(ang).astype(x.dtype)
    return jnp.concatenate([x1 * cos - x2 * sin, x1 * sin + x2 * cos], axis=-1)


def rope_2d(x, ang_r, ang_c):
    half = x.shape[-1] // 2
    return jnp.concatenate([rope_half(x[..., :half], ang_r), rope_half(x[..., half:], ang_c)], axis=-1)


def mixer_proj(h, w_in, g_qa, w_uq, g_kva, w_ukv, g_mq, g_mk, g_nq, g_nk):
    b_, s_ = h.shape[0], h.shape[1]
    p = h @ w_in
    gate_logits = p[..., :OFF_CQ]
    c_q = p[..., OFF_CQ:OFF_CKV]
    c_kv = p[..., OFF_CKV:OFF_KR]
    k_r = p[..., OFF_KR:OFF_NA]
    na = p[..., OFF_NA:].reshape(b_, s_, 3, NA_HEADS, NA_DIM)
    mq = (rms_norm(c_q, g_qa) @ w_uq).reshape(b_, s_, MLA_HEADS, MLA_QK)
    kv = (rms_norm(c_kv, g_kva) @ w_ukv).reshape(b_, s_, MLA_HEADS, MLA_NOPE + MLA_V)
    k_nope, mv = kv[..., :MLA_NOPE], kv[..., MLA_NOPE:]
    k_rope = jnp.broadcast_to(k_r[:, :, None, :], (b_, s_, MLA_HEADS, MLA_ROPE))
    mk = jnp.concatenate([k_nope, k_rope], axis=-1)
    mq = rms_norm(mq, g_mq)
    mk = rms_norm(mk, g_mk)
    nq = rms_norm(na[:, :, 0], g_nq)
    nk = rms_norm(na[:, :, 1], g_nk)
    nv = na[:, :, 2]
    return gate_logits, mq, mk, mv, nq, nk, nv


def dense_attention(q, k, v):
    scale = 1.0 / math.sqrt(q.shape[-1])
    s = jnp.einsum("bqhd,bkhd->bhqk", q, k).astype(jnp.float32) * scale
    p = jax.nn.softmax(s, axis=-1).astype(v.dtype)
    return jnp.einsum("bhqk,bkhe->bqhe", p, v)


def blocked_attention(q, k, v):
    b_, s_, h_, d_ = q.shape
    nb = s_ // Q_BLOCK
    qb = jnp.moveaxis(q.reshape(b_, nb, Q_BLOCK, h_, d_), 1, 0)
    ob = lax.map(lambda blk: dense_attention(blk, k, v), qb)
    return jnp.moveaxis(ob, 0, 1).reshape(b_, s_, h_, v.shape[-1])


def neighborhood_attention(q, k, v, k_ctx, v_ctx, rpb):
    b_, s_, h_, d_ = q.shape
    rows = s_ // GRID_W
    kh = min(NA_KH, rows)
    kw = NA_KW
    scale = 1.0 / math.sqrt(d_)
    to_grid = lambda t: t.reshape(b_, rows, GRID_W, h_, d_).transpose(1, 0, 3, 2, 4)
    q_g, k_g, v_g = to_grid(q), to_grid(k), to_grid(v)
    cols = jnp.arange(GRID_W)
    c_start = jnp.clip(cols - kw // 2, 0, GRID_W - kw)
    col_idx = c_start[:, None] + jnp.arange(kw)[None, :]
    dc_idx = col_idx - cols[:, None] + (NA_KW - 1)
    rpb_cols = rpb[:, :, dc_idx]

    def row_fn(args):
        q_r, r = args
        r_start = jnp.clip(r - kh // 2, 0, rows - kh)
        k_rows = lax.dynamic_slice_in_dim(k_g, r_start, kh, axis=0)
        v_rows = lax.dynamic_slice_in_dim(v_g, r_start, kh, axis=0)
        k_win = k_rows[:, :, :, col_idx]
        v_win = v_rows[:, :, :, col_idx]
        dr_idx = r_start + jnp.arange(kh) - r + (NA_KH - 1)
        bias = jnp.transpose(jnp.take(rpb_cols, dr_idx, axis=1), (0, 2, 1, 3))
        s_win = jnp.einsum("bhqd,ibhqjd->bhqij", q_r, k_win).astype(jnp.float32) * scale
        s_win = (s_win + bias.astype(jnp.float32)).reshape(b_, h_, GRID_W, kh * kw)
        s_ctx = jnp.einsum("bhqd,bkhd->bhqk", q_r, k_ctx).astype(jnp.float32) * scale
        p = jax.nn.softmax(jnp.concatenate([s_win, s_ctx], axis=-1), axis=-1).astype(v.dtype)
        p_win = p[..., :kh * kw].reshape(b_, h_, GRID_W, kh, kw)
        p_ctx = p[..., kh * kw:]
        return (jnp.einsum("bhqij,ibhqjd->bhqd", p_win, v_win)
                + jnp.einsum("bhqk,bkhd->bhqd", p_ctx, v_ctx))

    o = lax.map(row_fn, (q_g, jnp.arange(rows)))
    return o.transpose(1, 0, 3, 2, 4).reshape(b_, s_, h_ * d_)


def gated_merge(gate_logits, y_mla, y_na, w_mla_o, w_na_o, w_out):
    g = jax.nn.sigmoid(gate_logits.astype(jnp.float32)).astype(y_mla.dtype)
    y = g[..., :D_MODEL] * (y_mla @ w_mla_o) + g[..., D_MODEL:] * (y_na @ w_na_o)
    return y @ w_out


def squared_relu_mlp(h, w_ff1, w_ff2):
    return jnp.square(jax.nn.relu(h @ w_ff1)) @ w_ff2


def _normal(key, shape, scale):
    return jax.random.normal(key, shape, jnp.float32) * scale


def setup_inputs(seed: int = 0) -> dict:
    key = jax.random.key(seed)
    ks = jax.random.split(key, 24)
    L = DEPTH
    gain = lambda k, n: 1.0 + _normal(k, (L, n), 0.1)
    return {
        "x": _normal(ks[0], (BATCH, SEQ, D_MODEL), 1.0),
        "c": _normal(ks[1], (BATCH, D_MODEL), 1.0),
        "ctx": _normal(ks[2], (BATCH, CTX_LEN, D_MODEL), 1.0),
        "c_ctx": _normal(ks[3], (D_MODEL,), 1.0),
        "w_ada": _normal(ks[4], (L, D_MODEL, 6 * D_MODEL), D_MODEL ** -0.5),
        "b_ada": _normal(ks[5], (L, 6 * D_MODEL), 0.02),
        "g_attn": gain(ks[6], D_MODEL),
        "w_in": _normal(ks[7], (L, D_MODEL, IN_COLS), D_MODEL ** -0.5),
        "g_qa": gain(ks[8], Q_LORA),
        "w_uq": _normal(ks[9], (L, Q_LORA, MLA_HEADS * MLA_QK), Q_LORA ** -0.5),
        "g_kva": gain(ks[10], KV_LORA),
        "w_ukv": _normal(ks[11], (L, KV_LORA, MLA_HEADS * (MLA_NOPE + MLA_V)), KV_LORA ** -0.5),
        "g_mla_q": gain(ks[12], MLA_QK),
        "g_mla_k": gain(ks[13], MLA_QK),
        "g_na_q": gain(ks[14], NA_DIM),
        "g_na_k": gain(ks[15], NA_DIM),
        "rpb": _normal(ks[16], (L, NA_HEADS, 2 * NA_KH - 1, 2 * NA_KW - 1), 0.1),
        "w_mla_o": _normal(ks[17], (L, MLA_HEADS * MLA_V, D_MODEL), (MLA_HEADS * MLA_V) ** -0.5),
        "w_na_o": _normal(ks[18], (L, NA_HEADS * NA_DIM, D_MODEL), (NA_HEADS * NA_DIM) ** -0.5),
        "w_out": _normal(ks[19], (L, D_MODEL, D_MODEL), D_MODEL ** -0.5),
        "g_mlp": gain(ks[20], D_MODEL),
        "w_ff1": _normal(ks[21], (L, D_MODEL, D_FF), D_MODEL ** -0.5),
        "w_ff2": _normal(ks[22], (L, D_FF, D_MODEL), D_FF ** -0.5),
    }


def reference(x, c, ctx, c_ctx, w_ada, b_ada, g_attn, w_in, g_qa, w_uq, g_kva, w_ukv,
              g_mla_q, g_mla_k, g_na_q, g_na_k, rpb, w_mla_o, w_na_o, w_out,
              g_mlp, w_ff1, w_ff2):
    b_, s_, _ = x.shape
    l_ = ctx.shape[1]
    t = jnp.arange(s_)
    inv_freq = ROPE_BASE ** (-jnp.arange(ROPE_PAIRS, dtype=jnp.float32) / ROPE_PAIRS)
    ang_r = ((t // GRID_W).astype(jnp.float32)[:, None] * inv_freq)[:, None, :]
    ang_c = ((t % GRID_W).astype(jnp.float32)[:, None] * inv_freq)[:, None, :]
    silu_c = jax.nn.silu(c)
    silu_ctx = jax.nn.silu(c_ctx)
    x_lat, x_ctx = x, ctx
    for l in range(DEPTH):
        last = l == DEPTH - 1
        mod_lat = (silu_c @ w_ada[l] + b_ada[l])[:, None, :]
        mod_ctx = silu_ctx @ w_ada[l] + b_ada[l]
        sh1, sc1, gt1, sh2, sc2, gt2 = jnp.split(mod_lat, 6, axis=-1)
        csh1, csc1, cgt1, csh2, csc2, cgt2 = jnp.split(mod_ctx, 6, axis=-1)
        proj_w = (w_in[l], g_qa[l], w_uq[l], g_kva[l], w_ukv[l],
                  g_mla_q[l], g_mla_k[l], g_na_q[l], g_na_k[l])

        h_c = modulate(rms_norm(x_ctx, g_attn[l]), csh1, csc1)
        gl_c, mq_c, mk_c, mv_c, nq_c, nk_c, nv_c = mixer_proj(h_c, *proj_w)

        h = modulate(rms_norm(x_lat, g_attn[l]), sh1, sc1)
        gl, mq, mk, mv, nq, nk, nv = mixer_proj(h, *proj_w)
        mq = jnp.concatenate([mq[..., :MLA_NOPE], rope_2d(mq[..., MLA_NOPE:], ang_r, ang_c)], axis=-1)
        mk = jnp.concatenate([mk[..., :MLA_NOPE], rope_2d(mk[..., MLA_NOPE:], ang_r, ang_c)], axis=-1)
        y_mla = blocked_attention(mq, jnp.concatenate([mk_c, mk], axis=1),
                                  jnp.concatenate([mv_c, mv], axis=1)).reshape(b_, s_, MLA_HEADS * MLA_V)
        y_na = neighborhood_attention(nq, nk, nv, nk_c, nv_c, rpb[l])
        x_lat = x_lat + gt1 * gated_merge(gl, y_mla, y_na, w_mla_o[l], w_na_o[l], w_out[l])

        if not last:
            y_mla_c = dense_attention(mq_c, mk_c, mv_c).reshape(b_, l_, MLA_HEADS * MLA_V)
            y_na_c = dense_attention(nq_c, nk_c, nv_c).reshape(b_, l_, NA_HEADS * NA_DIM)
            x_ctx = x_ctx + cgt1 * gated_merge(gl_c, y_mla_c, y_na_c, w_mla_o[l], w_na_o[l], w_out[l])
            h2_c = modulate(rms_norm(x_ctx, g_mlp[l]), csh2, csc2)
            x_ctx = x_ctx + cgt2 * squared_relu_mlp(h2_c, w_ff1[l], w_ff2[l])

        h2 = modulate(rms_norm(x_lat, g_mlp[l]), sh2, sc2)
        x_lat = x_lat + gt2 * squared_relu_mlp(h2, w_ff1[l], w_ff2[l])
    return x_lat
```

```python
import functools
import math

import numpy as np
import jax
import jax.numpy as jnp
from jax import lax
from jax.experimental import pallas as pl
from jax.experimental.pallas import tpu as pltpu

D_MODEL = 1024
GRID_W = 64
MLA_HEADS = 8
MLA_NOPE = 64
MLA_ROPE = 32
MLA_V = 64
MLA_QK = MLA_NOPE + MLA_ROPE
Q_LORA = 768
KV_LORA = 256
NA_HEADS = 8
NA_DIM = 64
NA_KH = 8
NA_KW = 16
N_BRANCH = 2
ROPE_BASE = 10000.0
ROPE_PAIRS = MLA_ROPE // 4
EPS = 1e-6
OFF_CQ = N_BRANCH * D_MODEL
OFF_CKV = OFF_CQ + Q_LORA
OFF_KR = OFF_CKV + KV_LORA
OFF_NA = OFF_KR + MLA_ROPE

LANES = 128
ROW_TILE = 256
NA_BAND = 3 * ROW_TILE
NEG_BIG = -1e30
VMEM_LIMIT = 56 * 1024 * 1024

P_GATE = 0
P_CQ = OFF_CQ
P_CKV = P_CQ + Q_LORA
P_KR = P_CKV + KV_LORA
P_NA = P_KR + LANES
P_COLS = P_NA + 3 * NA_HEADS * NA_DIM

BF = jnp.bfloat16
F32 = jnp.float32


def _dot(a, b):
    return jnp.dot(a, b, preferred_element_type=F32)


def _dot_nt(a, b):
    return lax.dot_general(a, b, (((1,), (1,)), ((), ())), preferred_element_type=F32)


def _rope_swap_local():
    q = MLA_ROPE // 4
    rope = np.arange(MLA_ROPE)
    swapped = np.concatenate([rope[q:2 * q], rope[:q], rope[3 * q:], rope[2 * q:3 * q]])
    return np.concatenate([np.arange(MLA_NOPE), MLA_NOPE + rope, MLA_NOPE + swapped])


def _ada_kernel(c_ref, w_ref, b_ref, o_ref):
    c = c_ref[...]
    s = (c * jax.nn.sigmoid(c)).astype(BF)
    o_ref[...] = _dot(s, w_ref[...].astype(BF)) + b_ref[...]


def _ada_call(cc, w_ada, b_ada):
    depth = w_ada.shape[0]
    nrow = cc.shape[0]
    ncol = w_ada.shape[2]
    bn = 1024
    return pl.pallas_call(
        _ada_kernel,
        grid=(depth, ncol // bn),
        in_specs=[
            pl.BlockSpec((nrow, D_MODEL), lambda l, j: (0, 0)),
            pl.BlockSpec((None, D_MODEL, bn), lambda l, j: (l, 0, j)),
            pl.BlockSpec((None, 1, bn), lambda l, j: (l, 0, j)),
        ],
        out_specs=pl.BlockSpec((None, nrow, bn), lambda l, j: (l, 0, j)),
        out_shape=jax.ShapeDtypeStruct((depth, nrow, ncol), F32),
        compiler_params=pltpu.CompilerParams(
            dimension_semantics=("arbitrary", "arbitrary"), vmem_limit_bytes=VMEM_LIMIT),
        name="ada_mod",
    )(cc, w_ada, b_ada.reshape(depth, 1, ncol))


def _rms(x, g):
    ms = jnp.mean(x * x, axis=-1, keepdims=True)
    return x * lax.rsqrt(ms + EPS) * g


def _proj_kernel(x_ref, sh_ref, sc_ref, g_attn_ref, w_in_ref, g_qa_ref, w_uq_ref, g_kva_ref,
                 w_ukv_ref, g_mq_ref, g_mk_ref, g_nq_ref, g_nk_ref, cos_ref, sin_ref,
                 gate_ref, mq_ref, mk_ref, mv_ref, nq_ref, nk_ref, nv_ref):
    x = x_ref[...]
    h = _rms(x, g_attn_ref[...]) * (1.0 + sc_ref[...]) + sh_ref[...]
    hb = h.astype(BF)

    gate_chunk = 512
    for c0 in range(P_GATE, P_CQ, gate_chunk):
        logits = _dot(hb, w_in_ref[:, c0:c0 + gate_chunk])
        gate_ref[:, c0:c0 + gate_chunk] = jax.nn.sigmoid(logits).astype(BF)

    lane = lax.broadcasted_iota(jnp.int32, (1, LANES), 1)
    qk_mask = (lane < MLA_QK).astype(F32)
    cos = cos_ref[...]
    sin = sin_ref[...]

    def head_norm_rope(blk, g, out_scale):
        ss = jnp.sum(blk * blk * qk_mask, axis=-1, keepdims=True)
        y = blk * lax.rsqrt(ss * (1.0 / MLA_QK) + EPS) * g
        y = y * cos + pltpu.roll(y, LANES - MLA_ROPE, 1) * sin
        if out_scale != 1.0:
            y = y * out_scale
        return y.astype(BF)

    cq = _dot(hb, w_in_ref[:, P_CQ:P_CKV])
    mq = _dot(_rms(cq, g_qa_ref[...]).astype(BF), w_uq_ref[...])
    g_mq = g_mq_ref[...]
    for hd in range(MLA_HEADS):
        sl = slice(hd * LANES, (hd + 1) * LANES)
        mq_ref[:, sl] = head_norm_rope(mq[:, sl], g_mq, 1.0 / math.sqrt(MLA_QK))

    ckv = _dot(hb, w_in_ref[:, P_CKV:P_KR])
    ckvn = _rms(ckv, g_kva_ref[...]).astype(BF)
    kr = _dot(hb, w_in_ref[:, P_KR:P_NA])
    kk = _dot(ckvn, w_ukv_ref[:, :MLA_HEADS * LANES])
    mv_ref[...] = _dot(ckvn, w_ukv_ref[:, MLA_HEADS * LANES:]).astype(BF)
    g_mk = g_mk_ref[...]
    for hd in range(MLA_HEADS):
        sl = slice(hd * LANES, (hd + 1) * LANES)
        mk_ref[:, sl] = head_norm_rope(kk[:, sl] + kr, g_mk, 1.0)

    na = _dot(hb, w_in_ref[:, P_NA:P_COLS])
    width = NA_HEADS * NA_DIM
    lo_mask = (lane < NA_DIM).astype(F32)
    hi_mask = 1.0 - lo_mask

    def pair_norm(blk, g, out_scale):
        sq = blk * blk
        ss_lo = jnp.sum(sq * lo_mask, axis=-1, keepdims=True)
        ss_hi = jnp.sum(sq * hi_mask, axis=-1, keepdims=True)
        r = (lax.rsqrt(ss_lo * (1.0 / NA_DIM) + EPS) * lo_mask
             + lax.rsqrt(ss_hi * (1.0 / NA_DIM) + EPS) * hi_mask)
        return (blk * r * g * out_scale).astype(BF)

    g_nq = g_nq_ref[...]
    g_nk = g_nk_ref[...]
    for pr in range(width // LANES):
        sl = slice(pr * LANES, (pr + 1) * LANES)
        nq_ref[:, sl] = pair_norm(na[:, sl], g_nq, 1.0 / math.sqrt(NA_DIM))
        nk_ref[:, sl] = pair_norm(na[:, width + pr * LANES: width + (pr + 1) * LANES], g_nk, 1.0)
    nv_ref[...] = na[:, 2 * width:].astype(BF)


def _proj_call(x_all, mod, lw, cos_tab, sin_tab, n_lat_tiles, tiles_per_batch, n_batch):
    rows = x_all.shape[0]
    n_tiles = rows // ROW_TILE
    tm = ROW_TILE

    def mod_row(i):
        return jnp.where(i < n_lat_tiles, i // tiles_per_batch, n_batch)

    def rope_blk(i):
        return jnp.where(i < n_lat_tiles, i % tiles_per_batch, tiles_per_batch)

    full = lambda a: pl.BlockSpec(a.shape, lambda i: (0,) * a.ndim)
    in_specs = [
        pl.BlockSpec((tm, D_MODEL), lambda i: (i, 0)),
        pl.BlockSpec((None, 1, D_MODEL), lambda i: (mod_row(i), 0, 0)),
        pl.BlockSpec((None, 1, D_MODEL), lambda i: (mod_row(i), 0, 1)),
        full(lw["g_attn"]), full(lw["w_in"]), full(lw["g_qa"]), full(lw["w_uq"]),
        full(lw["g_kva"]), full(lw["w_ukv"]), full(lw["g_mq"]), full(lw["g_mk"]),
        full(lw["g_nq"]), full(lw["g_nk"]),
        pl.BlockSpec((tm, LANES), lambda i: (rope_blk(i), 0)),
        pl.BlockSpec((tm, LANES), lambda i: (rope_blk(i), 0)),
    ]
    widths = [N_BRANCH * D_MODEL, MLA_HEADS * LANES, MLA_HEADS * LANES, MLA_HEADS * MLA_V,
              NA_HEADS * NA_DIM, NA_HEADS * NA_DIM, NA_HEADS * NA_DIM]
    out_specs = [pl.BlockSpec((tm, w), lambda i: (i, 0)) for w in widths]
    out_shape = [jax.ShapeDtypeStruct((rows, w), BF) for w in widths]
    return pl.pallas_call(
        _proj_kernel,
        grid=(n_tiles,),
        in_specs=in_specs,
        out_specs=out_specs,
        out_shape=out_shape,
        compiler_params=pltpu.CompilerParams(
            dimension_semantics=("arbitrary",), vmem_limit_bytes=VMEM_LIMIT),
        name="mixer_proj",
    )(x_all, mod, mod, lw["g_attn"], lw["w_in"], lw["g_qa"], lw["w_uq"], lw["g_kva"],
      lw["w_ukv"], lw["g_mq"], lw["g_mk"], lw["g_nq"], lw["g_nk"], cos_tab, sin_tab)


def _softmax_pv(scores, values):
    m = functools.reduce(jnp.maximum, [jnp.max(s, axis=-1, keepdims=True) for s in scores])
    ps = [jnp.exp(s - m) for s in scores]
    denom = functools.reduce(lambda a, b: a + b, [jnp.sum(p, axis=-1, keepdims=True) for p in ps])
    o = functools.reduce(lambda a, b: a + b, [_dot(p.astype(BF), v) for p, v in zip(ps, values)])
    return o / denom


def _mla_kernel(q_ref, kl_ref, kc_ref, vl_ref, vc_ref, o_ref, *, n_lat_q):
    qi = pl.program_id(2)
    lane = lax.broadcasted_iota(jnp.int32, (1, LANES), 1)

    def run(with_latent):
        acc = None
        for hl in range(2):
            sl = slice(hl * LANES, (hl + 1) * LANES)
            q = q_ref[:, sl]
            scores = [_dot_nt(q, kc_ref[:, sl])]
            values = [vc_ref[...]]
            if with_latent:
                scores.append(_dot_nt(q, kl_ref[:, sl]))
                values.append(vl_ref[...])
            o = _softmax_pv(scores, values)
            acc = o if acc is None else jnp.where(lane < MLA_V, acc, o)
        o_ref[...] = acc.astype(o_ref.dtype)

    @pl.when(qi < n_lat_q)
    def _():
        run(True)

    @pl.when(qi == n_lat_q)
    def _():
        run(False)


def _mla_call(mq, mk, mv, n_batch, tiles_per_batch):
    rows = mq.shape[0]
    tq = ROW_TILE
    n_lat_tiles = n_batch * tiles_per_batch
    seq = tiles_per_batch * tq
    pairs = MLA_HEADS // 2

    def q_row(b, qi):
        return jnp.where(qi < tiles_per_batch, b * tiles_per_batch + qi, n_lat_tiles + b)

    return pl.pallas_call(
        functools.partial(_mla_kernel, n_lat_q=tiles_per_batch),
        grid=(n_batch, pairs, tiles_per_batch + 1),
        in_specs=[
            pl.BlockSpec((tq, 2 * LANES), lambda b, hp, qi: (q_row(b, qi), hp)),
            pl.BlockSpec((seq, 2 * LANES), lambda b, hp, qi: (b, hp)),
            pl.BlockSpec((tq, 2 * LANES), lambda b, hp, qi: (n_lat_tiles + b, hp)),
            pl.BlockSpec((seq, LANES), lambda b, hp, qi: (b, hp)),
            pl.BlockSpec((tq, LANES), lambda b, hp, qi: (n_lat_tiles + b, hp)),
        ],
        out_specs=pl.BlockSpec((tq, LANES), lambda b, hp, qi: (q_row(b, qi), hp)),
        out_shape=jax.ShapeDtypeStruct((rows, MLA_HEADS * MLA_V), BF),
        compiler_params=pltpu.CompilerParams(
            dimension_semantics=("arbitrary", "arbitrary", "arbitrary"),
            vmem_limit_bytes=VMEM_LIMIT),
        name="mla_attention",
    )(mq, mk, mk, mv, mv)


def _na_kernel(q_ref, kl_ref, kc_ref, vl_ref, vc_ref, bias_ref, o_ref, *, n_lat_q):
    j = pl.program_id(2)
    lane = lax.broadcasted_iota(jnp.int32, (1, LANES), 1)

    def run(with_latent):
        q = q_ref[...]
        if with_latent:
            start = pl.multiple_of(jnp.clip(j - 1, 0, n_lat_q - 3) * ROW_TILE, ROW_TILE)
            kb = kl_ref[pl.ds(start, NA_BAND), :]
            vb = vl_ref[pl.ds(start, NA_BAND), :]
            pat = jnp.where(j == 0, 0, jnp.where(j == n_lat_q - 1, 2, 1))
        acc = None
        for hl in range(2):
            in_head = (lane >= hl * NA_DIM) & (lane < (hl + 1) * NA_DIM)
            qh = jnp.where(in_head, q, jnp.zeros_like(q))
            scores = [_dot_nt(qh, kc_ref[...])]
            values = [vc_ref[...]]
            if with_latent:
                scores.append(_dot_nt(qh, kb) + bias_ref[pat, hl])
                values.append(vb)
            o = _softmax_pv(scores, values)
            acc = o if acc is None else jnp.where(lane < NA_DIM, acc, o)
        o_ref[...] = acc.astype(o_ref.dtype)

    @pl.when(j < n_lat_q)
    def _():
        run(True)

    @pl.when(j == n_lat_q)
    def _():
        run(False)


def _na_call(nq, nk, nv, bias, n_batch, tiles_per_batch):
    rows = nq.shape[0]
    tq = ROW_TILE
    n_lat_tiles = n_batch * tiles_per_batch
    seq = tiles_per_batch * tq
    pairs = NA_HEADS // 2

    def q_row(b, j):
        return jnp.where(j < tiles_per_batch, b * tiles_per_batch + j, n_lat_tiles + b)

    return pl.pallas_call(
        functools.partial(_na_kernel, n_lat_q=tiles_per_batch),
        grid=(pairs, n_batch, tiles_per_batch + 1),
        in_specs=[
            pl.BlockSpec((tq, LANES), lambda hp, b, j: (q_row(b, j), hp)),
            pl.BlockSpec((seq, LANES), lambda hp, b, j: (b, hp)),
            pl.BlockSpec((tq, LANES), lambda hp, b, j: (n_lat_tiles + b, hp)),
            pl.BlockSpec((seq, LANES), lambda hp, b, j: (b, hp)),
            pl.BlockSpec((tq, LANES), lambda hp, b, j: (n_lat_tiles + b, hp)),
            pl.BlockSpec((3, 2, tq, NA_BAND), lambda hp, b, j: (0, hp, 0, 0)),
        ],
        out_specs=pl.BlockSpec((tq, LANES), lambda hp, b, j: (q_row(b, j), hp)),
        out_shape=jax.ShapeDtypeStruct((rows, NA_HEADS * NA_DIM), BF),
        compiler_params=pltpu.CompilerParams(
            dimension_semantics=("arbitrary", "arbitrary", "arbitrary"),
            vmem_limit_bytes=VMEM_LIMIT),
        name="na_attention",
    )(nq, nk, nk, nv, nv, bias)


def _merge_kernel(x_ref, ym_ref, yn_ref, gate_ref, gt1_ref, w_mo_ref, w_no_ref, w_out_ref, o_ref):
    a = _dot(ym_ref[...], w_mo_ref[...])
    b = _dot(yn_ref[...], w_no_ref[...])
    y = gate_ref[:, :D_MODEL].astype(F32) * a + gate_ref[:, D_MODEL:].astype(F32) * b
    o = _dot(y.astype(BF), w_out_ref[...])
    o_ref[...] = x_ref[...] + gt1_ref[...] * o


def _mlp_kernel(x_ref, sh_ref, sc_ref, gt_ref, g_ref, w1_ref, w2_ref, o_ref):
    x = x_ref[...]
    h = (_rms(x, g_ref[...]) * (1.0 + sc_ref[...]) + sh_ref[...]).astype(BF)
    d_ff = w1_ref.shape[1]
    chunk = 1024
    acc = jnp.zeros_like(x)
    for c0 in range(0, d_ff, chunk):
        u = jnp.maximum(_dot(h, w1_ref[:, c0:c0 + chunk]), 0.0)
        acc = acc + _dot((u * u).astype(BF), w2_ref[c0:c0 + chunk, :])
    o_ref[...] = x + gt_ref[...] * acc


def _mod_spec(col, mod_row):
    return pl.BlockSpec((None, 1, D_MODEL), lambda i: (mod_row(i), 0, col))


def _merge_call(x_all, y_mla, y_na, gate, mod, lw, n_tiles, n_lat_tiles, tiles_per_batch, n_batch):
    tm = ROW_TILE
    mod_row = lambda i: jnp.where(i < n_lat_tiles, i // tiles_per_batch, n_batch)
    full = lambda a: pl.BlockSpec(a.shape, lambda i: (0,) * a.ndim)
    row = lambda w: pl.BlockSpec((tm, w), lambda i: (i, 0))
    return pl.pallas_call(
        _merge_kernel,
        grid=(n_tiles,),
        in_specs=[row(D_MODEL), row(y_mla.shape[1]), row(y_na.shape[1]), row(gate.shape[1]),
                  _mod_spec(2, mod_row), full(lw["w_mla_o"]), full(lw["w_na_o"]), full(lw["w_out"])],
        out_specs=row(D_MODEL),
        out_shape=jax.ShapeDtypeStruct((n_tiles * tm, D_MODEL), F32),
        compiler_params=pltpu.CompilerParams(
            dimension_semantics=("arbitrary",), vmem_limit_bytes=VMEM_LIMIT),
        name="merge_out",
    )(x_all, y_mla, y_na, gate, mod, lw["w_mla_o"], lw["w_na_o"], lw["w_out"])


def _mlp_call(x_all, mod, lw, n_tiles, n_lat_tiles, tiles_per_batch, n_batch):
    tm = ROW_TILE
    mod_row = lambda i: jnp.where(i < n_lat_tiles, i // tiles_per_batch, n_batch)
    full = lambda a: pl.BlockSpec(a.shape, lambda i: (0,) * a.ndim)
    row = lambda w: pl.BlockSpec((tm, w), lambda i: (i, 0))
    return pl.pallas_call(
        _mlp_kernel,
        grid=(n_tiles,),
        in_specs=[row(D_MODEL), _mod_spec(3, mod_row), _mod_spec(4, mod_row), _mod_spec(5, mod_row),
                  full(lw["g_mlp"]), full(lw["w_ff1"]), full(lw["w_ff2"])],
        out_specs=row(D_MODEL),
        out_shape=jax.ShapeDtypeStruct((n_tiles * tm, D_MODEL), F32),
        compiler_params=pltpu.CompilerParams(
            dimension_semantics=("arbitrary",), vmem_limit_bytes=VMEM_LIMIT),
        name="mlp",
    )(x_all, mod, mod, mod, lw["g_mlp"], lw["w_ff1"], lw["w_ff2"])


def _take_cols(w, idx):
    idx = np.asarray(idx)
    pieces = []
    start = 0
    while start < len(idx):
        stop = start + 1
        if idx[start] < 0:
            while stop < len(idx) and idx[stop] < 0:
                stop += 1
            pieces.append(jnp.zeros(w.shape[:-1] + (stop - start,), w.dtype))
        else:
            while stop < len(idx) and idx[stop] == idx[stop - 1] + 1:
                stop += 1
            pieces.append(w[..., int(idx[start]):int(idx[stop - 1]) + 1])
        start = stop
    return jnp.concatenate(pieces, axis=-1)


def _pack_weights(w_in, w_uq, w_ukv, g_mla_q, g_mla_k, g_na_q, g_na_k):
    local = _rope_swap_local()
    kr_local = local[MLA_NOPE:] - MLA_NOPE
    in_idx = np.concatenate([
        np.arange(OFF_KR),
        np.full(MLA_NOPE, -1), OFF_KR + kr_local,
        np.arange(OFF_NA, OFF_NA + 3 * NA_HEADS * NA_DIM)])
    uq_idx = np.concatenate([hd * MLA_QK + local for hd in range(MLA_HEADS)])
    per_head = MLA_NOPE + MLA_V
    ukv_k = np.concatenate([np.concatenate([hd * per_head + np.arange(MLA_NOPE),
                                            np.full(LANES - MLA_NOPE, -1)])
                            for hd in range(MLA_HEADS)])
    ukv_v = np.concatenate([hd * per_head + MLA_NOPE + np.arange(MLA_V) for hd in range(MLA_HEADS)])
    pair = np.concatenate([np.arange(NA_DIM), np.arange(NA_DIM)])
    return {
        "w_in": _take_cols(w_in, in_idx).astype(BF),
        "w_uq": _take_cols(w_uq, uq_idx).astype(BF),
        "w_ukv": _take_cols(w_ukv, np.concatenate([ukv_k, ukv_v])).astype(BF),
        "g_mq": _take_cols(g_mla_q, local)[:, None, :],
        "g_mk": _take_cols(g_mla_k, local)[:, None, :],
        "g_nq": _take_cols(g_na_q, pair)[:, None, :],
        "g_nk": _take_cols(g_na_k, pair)[:, None, :],
    }


def _rope_tables(seq):
    t = np.arange(seq)
    inv_freq = jnp.asarray(ROPE_BASE, F32) ** (-jnp.arange(ROPE_PAIRS, dtype=F32) / ROPE_PAIRS)
    ang_r = jnp.asarray(t // GRID_W, F32)[:, None] * inv_freq
    ang_c = jnp.asarray(t % GRID_W, F32)[:, None] * inv_freq
    cr, sr, cc, sn = jnp.cos(ang_r), jnp.sin(ang_r), jnp.cos(ang_c), jnp.sin(ang_c)
    ones = jnp.ones((seq, MLA_NOPE), F32)
    zeros = jnp.zeros((seq, MLA_NOPE), F32)
    pad = jnp.zeros((seq, LANES - MLA_QK), F32)
    cos_lat = jnp.concatenate([ones, cr, cr, cc, cc, pad], axis=1)
    sin_lat = jnp.concatenate([zeros, -sr, sr, -sn, sn, pad], axis=1)
    cos_id = jnp.concatenate([jnp.ones((ROW_TILE, MLA_QK), F32),
                              jnp.zeros((ROW_TILE, LANES - MLA_QK), F32)], axis=1)
    sin_id = jnp.zeros((ROW_TILE, LANES), F32)
    return jnp.concatenate([cos_lat, cos_id], axis=0), jnp.concatenate([sin_lat, sin_id], axis=0)


def _na_bias_tables(rpb, rows):
    rows_per_tile = ROW_TILE // GRID_W
    n_tiles = rows // rows_per_tile
    band_rows = NA_BAND // GRID_W
    kh = min(NA_KH, rows)
    assert band_rows >= kh + rows_per_tile - 1
    depth, heads = rpb.shape[:2]
    q_c = np.arange(GRID_W)[:, None]
    k_c = np.arange(GRID_W)[None, :]
    c_start = np.clip(q_c - NA_KW // 2, 0, GRID_W - NA_KW)
    col_ok = (k_c >= c_start) & (k_c < c_start + NA_KW)
    dc = np.clip(k_c - q_c + (NA_KW - 1), 0, 2 * NA_KW - 2)
    col_tab = rpb[:, :, :, jnp.asarray(dc)]
    q_r = np.arange(rows_per_tile)[:, None]
    k_r = np.arange(band_rows)[None, :]
    tabs = []
    for j in (0, 1, n_tiles - 1):
        band_start = int(np.clip(j - 1, 0, n_tiles - 3)) * rows_per_tile
        r = j * rows_per_tile + q_r
        r_start = np.clip(r - kh // 2, 0, rows - kh)
        key_row = band_start + k_r
        row_ok = (key_row >= r_start) & (key_row < r_start + kh)
        dr = np.clip(key_row - r + (NA_KH - 1), 0, 2 * NA_KH - 2)
        vals = jnp.take(col_tab, jnp.asarray(dr.reshape(-1)), axis=2)
        vals = vals.reshape(depth, heads, rows_per_tile, band_rows, GRID_W, GRID_W)
        ok = row_ok[:, :, None, None] & col_ok[None, None, :, :]
        vals = jnp.where(jnp.asarray(ok), vals, NEG_BIG)
        vals = vals.transpose(0, 1, 2, 4, 3, 5).reshape(depth, heads, ROW_TILE, NA_BAND)
        tabs.append(vals)
    return jnp.stack(tabs, axis=1).astype(F32)


def kernel(x, c, ctx, c_ctx, w_ada, b_ada, g_attn, w_in, g_qa, w_uq, g_kva, w_ukv, g_mla_q,
           g_mla_k, g_na_q, g_na_k, rpb, w_mla_o, w_na_o, w_out, g_mlp, w_ff1, w_ff2):
    n_batch, seq, d = x.shape
    ctx_len = ctx.shape[1]
    depth = w_ada.shape[0]
    assert d == D_MODEL and ctx_len == ROW_TILE and seq % ROW_TILE == 0
    tiles_per_batch = seq // ROW_TILE
    assert tiles_per_batch >= 3
    n_lat_tiles = n_batch * tiles_per_batch
    n_tiles = n_lat_tiles + n_batch
    rows_lat = n_batch * seq

    pad_rows = (-(n_batch + 1)) % 8
    cc = jnp.concatenate([c, c_ctx[None, :], jnp.zeros((pad_rows, d), F32)], axis=0)
    mod_all = _ada_call(cc, w_ada, b_ada)

    packed = _pack_weights(w_in, w_uq, w_ukv, g_mla_q, g_mla_k, g_na_q, g_na_k)
    cos_tab, sin_tab = _rope_tables(seq)
    bias_all = _na_bias_tables(rpb, seq // GRID_W)
    w_mla_o_b, w_na_o_b, w_out_b = w_mla_o.astype(BF), w_na_o.astype(BF), w_out.astype(BF)
    w_ff1_b, w_ff2_b = w_ff1.astype(BF), w_ff2.astype(BF)

    x_all = jnp.concatenate([x.reshape(rows_lat, d), ctx.reshape(n_batch * ctx_len, d)], axis=0)
    for l in range(depth):
        last = l == depth - 1
        lw = {
            "g_attn": g_attn[l][None, :], "w_in": packed["w_in"][l], "g_qa": g_qa[l][None, :],
            "w_uq": packed["w_uq"][l], "g_kva": g_kva[l][None, :], "w_ukv": packed["w_ukv"][l],
            "g_mq": packed["g_mq"][l], "g_mk": packed["g_mk"][l], "g_nq": packed["g_nq"][l],
            "g_nk": packed["g_nk"][l], "w_mla_o": w_mla_o_b[l], "w_na_o": w_na_o_b[l],
            "w_out": w_out_b[l], "g_mlp": g_mlp[l][None, :], "w_ff1": w_ff1_b[l],
            "w_ff2": w_ff2_b[l],
        }
        mod = mod_all[l].reshape(mod_all.shape[1], 1, 6 * d)
        gate, mq, mk, mv, nq, nk, nv = _proj_call(
            x_all, mod, lw, cos_tab, sin_tab, n_lat_tiles, tiles_per_batch, n_batch)
        y_mla = _mla_call(mq, mk, mv, n_batch, tiles_per_batch)
        y_na = _na_call(nq, nk, nv, bias_all[l], n_batch, tiles_per_batch)
        upd_tiles = n_lat_tiles if last else n_tiles
        x_all = _merge_call(x_all, y_mla, y_na, gate, mod, lw, upd_tiles, n_lat_tiles,
                            tiles_per_batch, n_batch)
        x_all = _mlp_call(x_all, mod, lw, upd_tiles, n_lat_tiles, tiles_per_batch, n_batch)
    return x_all[:rows_lat].reshape(n_batch, seq, d)
```

```python
import functools
import math

import numpy as np
import jax
import jax.numpy as jnp
from jax import lax
from jax.experimental import pallas as pl
from jax.experimental.pallas import tpu as pltpu

D_MODEL = 1024
GRID_W = 64
MLA_HEADS = 8
MLA_NOPE = 64
MLA_ROPE = 32
MLA_V = 64
MLA_QK = MLA_NOPE + MLA_ROPE
Q_LORA = 768
KV_LORA = 256
NA_HEADS = 8
NA_DIM = 64
NA_KH = 8
NA_KW = 16
N_BRANCH = 2
ROPE_BASE = 10000.0
ROPE_PAIRS = MLA_ROPE // 4
EPS = 1e-6
OFF_CQ = N_BRANCH * D_MODEL
OFF_CKV = OFF_CQ + Q_LORA
OFF_KR = OFF_CKV + KV_LORA
OFF_NA = OFF_KR + MLA_ROPE

LANES = 128
ROW_TILE = 256
NA_BAND = 3 * ROW_TILE
KEY_CHUNK = 512
NEG_BIG = -1e30
LOG2E = math.log2(math.e)
VMEM_LIMIT = 56 * 1024 * 1024

P_GATE = 0
P_CQ = OFF_CQ
P_CKV = P_CQ + Q_LORA
P_KR = P_CKV + KV_LORA
P_NA = P_KR + LANES
P_COLS = P_NA + 3 * NA_HEADS * NA_DIM

BF = jnp.bfloat16
F32 = jnp.float32


def _dot(a, b):
    return jnp.dot(a, b, preferred_element_type=F32)


def _dot_nt(a, b):
    return lax.dot_general(a, b, (((1,), (1,)), ((), ())), preferred_element_type=F32)


def _dot_tn(a, b):
    return lax.dot_general(a, b, (((0,), (0,)), ((), ())), preferred_element_type=F32)


def _rope_swap_local():
    q = MLA_ROPE // 4
    rope = np.arange(MLA_ROPE)
    swapped = np.concatenate([rope[q:2 * q], rope[:q], rope[3 * q:], rope[2 * q:3 * q]])
    return np.concatenate([np.arange(MLA_NOPE), MLA_NOPE + rope, MLA_NOPE + swapped])


def _single(block_shape, index_map):
    return pl.BlockSpec(block_shape, index_map, pipeline_mode=pl.Buffered(1))


def _ada_kernel(c_ref, w_ref, b_ref, o_ref):
    c = c_ref[...]
    s = (c * jax.nn.sigmoid(c)).astype(BF)
    o_ref[...] = _dot(s, w_ref[...].astype(BF)) + b_ref[...]


def _ada_call(cc, w_ada, b_ada):
    depth = w_ada.shape[0]
    nrow = cc.shape[0]
    ncol = w_ada.shape[2]
    bn = 1024
    return pl.pallas_call(
        _ada_kernel,
        grid=(depth, ncol // bn),
        in_specs=[
            pl.BlockSpec((nrow, D_MODEL), lambda l, j: (0, 0)),
            pl.BlockSpec((None, D_MODEL, bn), lambda l, j: (l, 0, j)),
            pl.BlockSpec((None, 1, bn), lambda l, j: (l, 0, j)),
        ],
        out_specs=pl.BlockSpec((None, nrow, bn), lambda l, j: (l, 0, j)),
        out_shape=jax.ShapeDtypeStruct((depth, nrow, ncol), F32),
        compiler_params=pltpu.CompilerParams(
            dimension_semantics=("arbitrary", "arbitrary"), vmem_limit_bytes=VMEM_LIMIT),
        name="ada_mod",
    )(cc, w_ada, b_ada.reshape(depth, 1, ncol))


def _rms(x, g):
    ms = jnp.mean(x * x, axis=-1, keepdims=True)
    return x * lax.rsqrt(ms + EPS) * g


def _proj_kernel(x_ref, sh_ref, sc_ref, g_attn_ref, w_in_ref, g_qa_ref, w_uq_ref, g_kva_ref,
                 w_ukv_ref, g_nq_ref, g_nk_ref, qcos_ref, qsin_ref, kcos_ref, ksin_ref,
                 gate_ref, mq_ref, mk_ref, mv_ref, nq_ref, nk_ref, nv_ref):
    x = x_ref[...]
    h = _rms(x, g_attn_ref[...] * (1.0 + sc_ref[...])) + sh_ref[...]
    hb = h.astype(BF)

    lane = lax.broadcasted_iota(jnp.int32, (1, LANES), 1)
    qk_mean = jnp.where(lane < MLA_QK, 1.0 / MLA_QK, 0.0).astype(F32)

    def head_norm_rope(blk, cos, sin):
        ms = jnp.sum(blk * blk * qk_mean, axis=-1, keepdims=True)
        y = blk * cos + pltpu.roll(blk, LANES - MLA_ROPE, 1) * sin
        return (y * lax.rsqrt(ms + EPS)).astype(BF)

    cq = _dot(hb, w_in_ref[:, P_CQ:P_CKV])
    mq = _dot(_rms(cq, g_qa_ref[...]).astype(BF), w_uq_ref[...])
    qcos = qcos_ref[...]
    qsin = qsin_ref[...]
    for hd in range(MLA_HEADS):
        sl = slice(hd * LANES, (hd + 1) * LANES)
        mq_ref[:, sl] = head_norm_rope(mq[:, sl], qcos, qsin)

    ckv = _dot(hb, w_in_ref[:, P_CKV:P_KR])
    ckvn = _rms(ckv, g_kva_ref[...]).astype(BF)
    kr = _dot(hb, w_in_ref[:, P_KR:P_NA])
    kk = _dot(ckvn, w_ukv_ref[:, :MLA_HEADS * LANES])
    mv_ref[...] = _dot(ckvn, w_ukv_ref[:, MLA_HEADS * LANES:]).astype(BF)
    kcos = kcos_ref[...]
    ksin = ksin_ref[...]
    for hd in range(MLA_HEADS):
        sl = slice(hd * LANES, (hd + 1) * LANES)
        mk_ref[:, sl] = head_norm_rope(kk[:, sl] + kr, kcos, ksin)

    na = _dot(hb, w_in_ref[:, P_NA:P_COLS])
    width = NA_HEADS * NA_DIM
    is_lo = lane < NA_DIM
    lo_mean = jnp.where(is_lo, 1.0 / NA_DIM, 0.0).astype(F32)
    hi_mean = jnp.where(is_lo, 0.0, 1.0 / NA_DIM).astype(F32)

    def pair_norm(blk, g):
        sq = blk * blk
        ms_lo = jnp.sum(sq * lo_mean, axis=-1, keepdims=True)
        ms_hi = jnp.sum(sq * hi_mean, axis=-1, keepdims=True)
        r = lax.rsqrt(jnp.where(is_lo, ms_lo, ms_hi) + EPS)
        return (blk * g * r).astype(BF)

    g_nq = g_nq_ref[...]
    g_nk = g_nk_ref[...]
    for pr in range(width // LANES):
        sl = slice(pr * LANES, (pr + 1) * LANES)
        nq_ref[:, sl] = pair_norm(na[:, sl], g_nq)
        nk_ref[:, sl] = pair_norm(na[:, width + pr * LANES: width + (pr + 1) * LANES], g_nk)
    nv_ref[...] = na[:, 2 * width:].astype(BF)

    gate_chunk = 512
    for c0 in range(P_GATE, P_CQ, gate_chunk):
        logits = _dot(hb, w_in_ref[:, c0:c0 + gate_chunk])
        gate_ref[:, c0:c0 + gate_chunk] = jax.nn.sigmoid(logits).astype(BF)


def _proj_call(x_all, mod, lw, n_lat_tiles, tiles_per_batch, n_batch):
    rows = x_all.shape[0]
    n_tiles = rows // ROW_TILE
    tm = ROW_TILE

    def mod_row(i):
        return jnp.where(i < n_lat_tiles, i // tiles_per_batch, n_batch)

    def rope_blk(i):
        return jnp.where(i < n_lat_tiles, i % tiles_per_batch, tiles_per_batch)

    full = lambda a: _single(a.shape, lambda i: (0,) * a.ndim)
    rope = pl.BlockSpec((tm, LANES), lambda i: (rope_blk(i), 0))
    in_specs = [
        pl.BlockSpec((tm, D_MODEL), lambda i: (i, 0)),
        pl.BlockSpec((None, 1, D_MODEL), lambda i: (mod_row(i), 0, 0)),
        pl.BlockSpec((None, 1, D_MODEL), lambda i: (mod_row(i), 0, 1)),
        full(lw["g_attn"]), full(lw["w_in"]), full(lw["g_qa"]), full(lw["w_uq"]),
        full(lw["g_kva"]), full(lw["w_ukv"]), full(lw["g_nq"]), full(lw["g_nk"]),
        rope, rope, rope, rope,
    ]
    widths = [N_BRANCH * D_MODEL, MLA_HEADS * LANES, MLA_HEADS * LANES, MLA_HEADS * MLA_V,
              NA_HEADS * NA_DIM, NA_HEADS * NA_DIM, NA_HEADS * NA_DIM]
    out_specs = [pl.BlockSpec((tm, w), lambda i: (i, 0)) for w in widths]
    out_shape = [jax.ShapeDtypeStruct((rows, w), BF) for w in widths]
    return pl.pallas_call(
        _proj_kernel,
        grid=(n_tiles,),
        in_specs=in_specs,
        out_specs=out_specs,
        out_shape=out_shape,
        compiler_params=pltpu.CompilerParams(
            dimension_semantics=("arbitrary",), vmem_limit_bytes=VMEM_LIMIT),
        name="mixer_proj",
    )(x_all, mod, mod, lw["g_attn"], lw["w_in"], lw["g_qa"], lw["w_uq"], lw["g_kva"],
      lw["w_ukv"], lw["g_nq"], lw["g_nk"], lw["qcos"], lw["qsin"], lw["kcos"], lw["ksin"])


def _flash_transposed(qs, chunk_lists):
    n_streams = len(qs)
    n_chunks = len(chunk_lists[0])

    def scores(i, c):
        k, _, bias = chunk_lists[i][c]
        s = _dot_nt(k, qs[i])
        return s if bias is None else s + bias

    state = [None] * n_streams
    s_cur = [scores(i, 0) for i in range(n_streams)]
    for c in range(n_chunks):
        s_next = [scores(i, c + 1) for i in range(n_streams)] if c + 1 < n_chunks else None
        for i in range(n_streams):
            s = s_cur[i]
            v = chunk_lists[i][c][1]
            m_c = jnp.max(s, axis=0, keepdims=True)
            if state[i] is None:
                m_new = m_c
                p = jnp.exp2(s - m_new)
                l = jnp.sum(p, axis=0, keepdims=True)
                acc = _dot_tn(v, p.astype(BF))
            else:
                m, l, acc = state[i]
                m_new = jnp.maximum(m, m_c)
                alpha = jnp.exp2(m - m_new)
                p = jnp.exp2(s - m_new)
                l = alpha * l + jnp.sum(p, axis=0, keepdims=True)
                acc = alpha * acc + _dot_tn(v, p.astype(BF))
            state[i] = (m_new, l, acc)
        s_cur = s_next
    return [acc * (1.0 / l) for _, l, acc in state]


def _merge_pair(o_t0, o_t1, head_dim):
    row = lax.broadcasted_iota(jnp.int32, (LANES, 1), 0)
    return jnp.where(row < head_dim, o_t0, o_t1).T


def _mla_kernel(q_ref, kl_ref, kc_ref, vl_ref, vc_ref, o_ref, *, n_lat_q):
    qi = pl.program_id(2)
    seq = kl_ref.shape[0]

    def run(with_latent):
        qs, chunk_lists = [], []
        for hl in range(2):
            sl = slice(hl * LANES, (hl + 1) * LANES)
            chunks = [(kc_ref[:, sl], vc_ref[...], None)]
            if with_latent:
                for c0 in range(0, seq, KEY_CHUNK):
                    rows = slice(c0, c0 + KEY_CHUNK)
                    chunks.append((kl_ref[rows, sl], vl_ref[rows, :], None))
            qs.append(q_ref[:, sl])
            chunk_lists.append(chunks)
        outs = _flash_transposed(qs, chunk_lists)
        o_ref[...] = _merge_pair(outs[0], outs[1], MLA_V).astype(o_ref.dtype)

    @pl.when(qi < n_lat_q)
    def _():
        run(True)

    @pl.when(qi == n_lat_q)
    def _():
        run(False)


def _mla_call(mq, mk, mv, n_batch, tiles_per_batch):
    rows = mq.shape[0]
    tq = ROW_TILE
    n_lat_tiles = n_batch * tiles_per_batch
    seq = tiles_per_batch * tq
    pairs = MLA_HEADS // 2

    def q_row(b, qi):
        return jnp.where(qi < tiles_per_batch, b * tiles_per_batch + qi, n_lat_tiles + b)

    return pl.pallas_call(
        functools.partial(_mla_kernel, n_lat_q=tiles_per_batch),
        grid=(n_batch, pairs, tiles_per_batch + 1),
        in_specs=[
            pl.BlockSpec((tq, 2 * LANES), lambda b, hp, qi: (q_row(b, qi), hp)),
            pl.BlockSpec((seq, 2 * LANES), lambda b, hp, qi: (b, hp)),
            pl.BlockSpec((tq, 2 * LANES), lambda b, hp, qi: (n_lat_tiles + b, hp)),
            pl.BlockSpec((seq, LANES), lambda b, hp, qi: (b, hp)),
            pl.BlockSpec((tq, LANES), lambda b, hp, qi: (n_lat_tiles + b, hp)),
        ],
        out_specs=pl.BlockSpec((tq, LANES), lambda b, hp, qi: (q_row(b, qi), hp)),
        out_shape=jax.ShapeDtypeStruct((rows, MLA_HEADS * MLA_V), BF),
        compiler_params=pltpu.CompilerParams(
            dimension_semantics=("arbitrary", "arbitrary", "arbitrary"),
            vmem_limit_bytes=VMEM_LIMIT),
        name="mla_attention",
    )(mq, mk, mk, mv, mv)


def _na_kernel(q_ref, kl_ref, kc_ref, vl_ref, vc_ref, bias_ref, o_ref, *, n_lat_q):
    j = pl.program_id(2)
    lane = lax.broadcasted_iota(jnp.int32, (1, LANES), 1)

    def run(with_latent):
        q = q_ref[...]
        if with_latent:
            start = pl.multiple_of(jnp.clip(j - 1, 0, n_lat_q - 3) * ROW_TILE, ROW_TILE)
            pat = jnp.where(j == 0, 0, jnp.where(j == n_lat_q - 1, 2, 1))
        qs, chunk_lists = [], []
        for hl in range(2):
            in_head = (lane >= hl * NA_DIM) & (lane < (hl + 1) * NA_DIM)
            chunks = [(kc_ref[...], vc_ref[...], None)]
            if with_latent:
                for c in range(NA_BAND // ROW_TILE):
                    rows = pl.ds(pl.multiple_of(start + c * ROW_TILE, ROW_TILE), ROW_TILE)
                    bias = bias_ref[pat, hl, c * ROW_TILE:(c + 1) * ROW_TILE, :]
                    chunks.append((kl_ref[rows, :], vl_ref[rows, :], bias))
            qs.append(jnp.where(in_head, q, jnp.zeros_like(q)))
            chunk_lists.append(chunks)
        outs = _flash_transposed(qs, chunk_lists)
        o_ref[...] = _merge_pair(outs[0], outs[1], NA_DIM).astype(o_ref.dtype)

    @pl.when(j < n_lat_q)
    def _():
        run(True)

    @pl.when(j == n_lat_q)
    def _():
        run(False)


def _na_call(nq, nk, nv, bias, n_batch, tiles_per_batch):
    rows = nq.shape[0]
    tq = ROW_TILE
    n_lat_tiles = n_batch * tiles_per_batch
    seq = tiles_per_batch * tq
    pairs = NA_HEADS // 2

    def q_row(b, j):
        return jnp.where(j < tiles_per_batch, b * tiles_per_batch + j, n_lat_tiles + b)

    return pl.pallas_call(
        functools.partial(_na_kernel, n_lat_q=tiles_per_batch),
        grid=(pairs, n_batch, tiles_per_batch + 1),
        in_specs=[
            pl.BlockSpec((tq, LANES), lambda hp, b, j: (q_row(b, j), hp)),
            pl.BlockSpec((seq, LANES), lambda hp, b, j: (b, hp)),
            pl.BlockSpec((tq, LANES), lambda hp, b, j: (n_lat_tiles + b, hp)),
            pl.BlockSpec((seq, LANES), lambda hp, b, j: (b, hp)),
            pl.BlockSpec((tq, LANES), lambda hp, b, j: (n_lat_tiles + b, hp)),
            pl.BlockSpec((3, 2, NA_BAND, tq), lambda hp, b, j: (0, hp, 0, 0)),
        ],
        out_specs=pl.BlockSpec((tq, LANES), lambda hp, b, j: (q_row(b, j), hp)),
        out_shape=jax.ShapeDtypeStruct((rows, NA_HEADS * NA_DIM), BF),
        compiler_params=pltpu.CompilerParams(
            dimension_semantics=("arbitrary", "arbitrary", "arbitrary"),
            vmem_limit_bytes=VMEM_LIMIT),
        name="na_attention",
    )(nq, nk, nk, nv, nv, bias)


def _na_window_plan(rows):
    rows_per_tile = ROW_TILE // GRID_W
    n_tiles = rows // rows_per_tile
    band_rows = NA_BAND // GRID_W
    kh = min(NA_KH, rows)
    assert band_rows >= kh + rows_per_tile - 1 and rows_per_tile % 2 == 0
    plan = []
    for j in (0, 1, n_tiles - 1):
        band_start = int(np.clip(j - 1, 0, n_tiles - 3)) * rows_per_tile
        tile_plan = []
        for k_r in range(band_rows):
            key_row = band_start + k_r
            for qp in range(rows_per_tile // 2):
                ok = []
                for r in (j * rows_per_tile + 2 * qp, j * rows_per_tile + 2 * qp + 1):
                    r_start = int(np.clip(r - kh // 2, 0, rows - kh))
                    ok.append(r_start <= key_row < r_start + kh)
                d_left = key_row - (j * rows_per_tile + 2 * qp) + (NA_KH - 1)
                tile_plan.append((k_r, qp, int(np.clip(d_left, 0, 2 * NA_KH - 1)), ok[0], ok[1]))
        plan.append(tile_plan)
    return plan


def _na_bias_kernel(tab_ref, o_ref, *, plan):
    lane = lax.broadcasted_iota(jnp.int32, (1, 2 * GRID_W), 1)
    neg = jnp.full((GRID_W, 2 * GRID_W), NEG_BIG, F32)
    for pi, tile_plan in enumerate(plan):
        for k_r, qp, d_left, left_ok, right_ok in tile_plan:
            if left_ok and right_ok:
                blk = tab_ref[d_left]
            elif left_ok:
                blk = jnp.where(lane < GRID_W, tab_ref[d_left], NEG_BIG)
            elif right_ok:
                blk = jnp.where(lane < GRID_W, NEG_BIG, tab_ref[d_left])
            else:
                blk = neg
            o_ref[pi, k_r * GRID_W:(k_r + 1) * GRID_W, qp * 2 * GRID_W:(qp + 1) * 2 * GRID_W] = blk


def _na_bias_tables(rpb, rows):
    depth, heads = rpb.shape[:2]
    n_dr = 2 * NA_KH - 1
    k_c = np.arange(GRID_W)[:, None]
    q_c = np.arange(GRID_W)[None, :]
    c_start = np.clip(q_c - NA_KW // 2, 0, GRID_W - NA_KW)
    col_ok = (k_c >= c_start) & (k_c < c_start + NA_KW)
    dc = np.clip(k_c - q_c + (NA_KW - 1), 0, 2 * NA_KW - 2)
    col_tab = jnp.where(jnp.asarray(col_ok), rpb[:, :, :, jnp.asarray(dc)] * LOG2E, NEG_BIG)
    neg = jnp.full((depth, heads, 1, GRID_W, GRID_W), NEG_BIG, F32)
    pair_tab = jnp.concatenate([jnp.concatenate([col_tab, neg], axis=2),
                                jnp.concatenate([neg, col_tab], axis=2)], axis=-1)
    plan = _na_window_plan(rows)
    return pl.pallas_call(
        functools.partial(_na_bias_kernel, plan=plan),
        grid=(depth, heads),
        in_specs=[pl.BlockSpec((None, None, n_dr + 1, GRID_W, 2 * GRID_W),
                               lambda l, h: (l, h, 0, 0, 0))],
        out_specs=pl.BlockSpec((None, 3, None, NA_BAND, ROW_TILE), lambda l, h: (l, 0, h, 0, 0)),
        out_shape=jax.ShapeDtypeStruct((depth, 3, heads, NA_BAND, ROW_TILE), F32),
        compiler_params=pltpu.CompilerParams(
            dimension_semantics=("arbitrary", "arbitrary"), vmem_limit_bytes=VMEM_LIMIT),
        name="na_bias_tables",
    )(pair_tab)


def _merge_mlp_kernel(x_ref, ym_ref, yn_ref, gate_ref, gt1_ref, sh_ref, sc_ref, gt2_ref,
                      w_mo_ref, w_no_ref, w_out_ref, g_ref, w1_ref, w2_ref, o_ref):
    a = _dot(ym_ref[...], w_mo_ref[...])
    b = _dot(yn_ref[...], w_no_ref[...])
    y = gate_ref[:, :D_MODEL].astype(F32) * a + gate_ref[:, D_MODEL:].astype(F32) * b
    x = x_ref[...] + gt1_ref[...] * _dot(y.astype(BF), w_out_ref[...])

    h = (_rms(x, g_ref[...] * (1.0 + sc_ref[...])) + sh_ref[...]).astype(BF)
    d_ff = w1_ref.shape[1]
    chunk = 1024
    acc = None
    for c0 in range(0, d_ff, chunk):
        u = jnp.maximum(_dot(h, w1_ref[:, c0:c0 + chunk]), 0.0)
        part = _dot((u * u).astype(BF), w2_ref[c0:c0 + chunk, :])
        acc = part if acc is None else acc + part
    o_ref[...] = x + gt2_ref[...] * acc


def _merge_mlp_call(x_all, y_mla, y_na, gate, mod, lw, n_tiles, n_lat_tiles, tiles_per_batch,
                    n_batch):
    tm = ROW_TILE
    mod_row = lambda i: jnp.where(i < n_lat_tiles, i // tiles_per_batch, n_batch)
    mod_spec = lambda col: pl.BlockSpec((None, 1, D_MODEL), lambda i: (mod_row(i), 0, col))
    full = lambda a: _single(a.shape, lambda i: (0,) * a.ndim)
    row = lambda w: pl.BlockSpec((tm, w), lambda i: (i, 0))
    return pl.pallas_call(
        _merge_mlp_kernel,
        grid=(n_tiles,),
        in_specs=[row(D_MODEL), row(y_mla.shape[1]), row(y_na.shape[1]), row(gate.shape[1]),
                  mod_spec(2), mod_spec(3), mod_spec(4), mod_spec(5),
                  full(lw["w_mla_o"]), full(lw["w_na_o"]), full(lw["w_out"]),
                  full(lw["g_mlp"]), full(lw["w_ff1"]), full(lw["w_ff2"])],
        out_specs=row(D_MODEL),
        out_shape=jax.ShapeDtypeStruct((n_tiles * tm, D_MODEL), F32),
        compiler_params=pltpu.CompilerParams(
            dimension_semantics=("arbitrary",), vmem_limit_bytes=VMEM_LIMIT),
        name="merge_mlp",
    )(x_all, y_mla, y_na, gate, mod, mod, mod, mod, lw["w_mla_o"], lw["w_na_o"], lw["w_out"],
      lw["g_mlp"], lw["w_ff1"], lw["w_ff2"])


def _take_cols(w, idx):
    idx = np.asarray(idx)
    pieces = []
    start = 0
    while start < len(idx):
        stop = start + 1
        if idx[start] < 0:
            while stop < len(idx) and idx[stop] < 0:
                stop += 1
            pieces.append(jnp.zeros(w.shape[:-1] + (stop - start,), w.dtype))
        else:
            while stop < len(idx) and idx[stop] == idx[stop - 1] + 1:
                stop += 1
            pieces.append(w[..., int(idx[start]):int(idx[stop - 1]) + 1])
        start = stop
    return jnp.concatenate(pieces, axis=-1)


def _pack_weights(w_in, w_uq, w_ukv, g_mla_q, g_mla_k, g_na_q, g_na_k, seq):
    local = _rope_swap_local()
    kr_local = local[MLA_NOPE:] - MLA_NOPE
    in_idx = np.concatenate([
        np.arange(OFF_KR),
        np.full(MLA_NOPE, -1), OFF_KR + kr_local,
        np.arange(OFF_NA, OFF_NA + 3 * NA_HEADS * NA_DIM)])
    uq_idx = np.concatenate([hd * MLA_QK + local for hd in range(MLA_HEADS)])
    per_head = MLA_NOPE + MLA_V
    ukv_k = np.concatenate([np.concatenate([hd * per_head + np.arange(MLA_NOPE),
                                            np.full(LANES - MLA_NOPE, -1)])
                            for hd in range(MLA_HEADS)])
    ukv_v = np.concatenate([hd * per_head + MLA_NOPE + np.arange(MLA_V) for hd in range(MLA_HEADS)])
    pair = np.concatenate([np.arange(NA_DIM), np.arange(NA_DIM)])

    cos_base, sin_base = _rope_tables(seq)
    partner = (np.arange(LANES) + MLA_ROPE) % LANES

    def rope_tabs(g, scale):
        g_ext = _take_cols(g, local)
        g_partner = _take_cols(g, local[partner])
        return (cos_base[None] * (g_ext * scale)[:, None, :],
                sin_base[None] * (g_partner * scale)[:, None, :])

    qcos, qsin = rope_tabs(g_mla_q, LOG2E / math.sqrt(MLA_QK))
    kcos, ksin = rope_tabs(g_mla_k, 1.0)
    return {
        "w_in": _take_cols(w_in.astype(BF), in_idx),
        "w_uq": _take_cols(w_uq.astype(BF), uq_idx),
        "w_ukv": _take_cols(w_ukv.astype(BF), np.concatenate([ukv_k, ukv_v])),
        "qcos": qcos, "qsin": qsin, "kcos": kcos, "ksin": ksin,
        "g_nq": (_take_cols(g_na_q, pair) * (LOG2E / math.sqrt(NA_DIM)))[:, None, :],
        "g_nk": _take_cols(g_na_k, pair)[:, None, :],
    }


def _rope_tables(seq):
    t = np.arange(seq)
    inv_freq = jnp.asarray(ROPE_BASE, F32) ** (-jnp.arange(ROPE_PAIRS, dtype=F32) / ROPE_PAIRS)
    ang_r = jnp.asarray(t // GRID_W, F32)[:, None] * inv_freq
    ang_c = jnp.asarray(t % GRID_W, F32)[:, None] * inv_freq
    cr, sr, cc, sn = jnp.cos(ang_r), jnp.sin(ang_r), jnp.cos(ang_c), jnp.sin(ang_c)
    ones = jnp.ones((seq, MLA_NOPE), F32)
    zeros = jnp.zeros((seq, MLA_NOPE), F32)
    pad = jnp.zeros((seq, LANES - MLA_QK), F32)
    cos_lat = jnp.concatenate([ones, cr, cr, cc, cc, pad], axis=1)
    sin_lat = jnp.concatenate([zeros, -sr, sr, -sn, sn, pad], axis=1)
    cos_id = jnp.concatenate([jnp.ones((ROW_TILE, MLA_QK), F32),
                              jnp.zeros((ROW_TILE, LANES - MLA_QK), F32)], axis=1)
    sin_id = jnp.zeros((ROW_TILE, LANES), F32)
    return jnp.concatenate([cos_lat, cos_id], axis=0), jnp.concatenate([sin_lat, sin_id], axis=0)


def kernel(x, c, ctx, c_ctx, w_ada, b_ada, g_attn, w_in, g_qa, w_uq, g_kva, w_ukv, g_mla_q,
           g_mla_k, g_na_q, g_na_k, rpb, w_mla_o, w_na_o, w_out, g_mlp, w_ff1, w_ff2):
    n_batch, seq, d = x.shape
    ctx_len = ctx.shape[1]
    depth = w_ada.shape[0]
    assert d == D_MODEL and ctx_len == ROW_TILE and seq % KEY_CHUNK == 0
    tiles_per_batch = seq // ROW_TILE
    assert tiles_per_batch >= 3
    n_lat_tiles = n_batch * tiles_per_batch
    n_tiles = n_lat_tiles + n_batch
    rows_lat = n_batch * seq

    pad_rows = (-(n_batch + 1)) % 8
    cc = jnp.concatenate([c, c_ctx[None, :], jnp.zeros((pad_rows, d), F32)], axis=0)
    mod_all = _ada_call(cc, w_ada, b_ada)

    packed = _pack_weights(w_in, w_uq, w_ukv, g_mla_q, g_mla_k, g_na_q, g_na_k, seq)
    bias_all = _na_bias_tables(rpb, seq // GRID_W)
    w_mla_o_b, w_na_o_b, w_out_b = w_mla_o.astype(BF), w_na_o.astype(BF), w_out.astype(BF)
    w_ff1_b, w_ff2_b = w_ff1.astype(BF), w_ff2.astype(BF)

    x_all = jnp.concatenate([x.reshape(rows_lat, d), ctx.reshape(n_batch * ctx_len, d)], axis=0)
    for l in range(depth):
        last = l == depth - 1
        lw = {
            "g_attn": g_attn[l][None, :], "g_qa": g_qa[l][None, :], "g_kva": g_kva[l][None, :],
            "w_mla_o": w_mla_o_b[l], "w_na_o": w_na_o_b[l], "w_out": w_out_b[l],
            "g_mlp": g_mlp[l][None, :], "w_ff1": w_ff1_b[l], "w_ff2": w_ff2_b[l],
        }
        lw.update({name: tab[l] for name, tab in packed.items()})
        mod = mod_all[l].reshape(mod_all.shape[1], 1, 6 * d)
        gate, mq, mk, mv, nq, nk, nv = _proj_call(
            x_all, mod, lw, n_lat_tiles, tiles_per_batch, n_batch)
        y_mla = _mla_call(mq, mk, mv, n_batch, tiles_per_batch)
        y_na = _na_call(nq, nk, nv, bias_all[l], n_batch, tiles_per_batch)
        upd_tiles = n_lat_tiles if last else n_tiles
        x_all = _merge_mlp_call(x_all, y_mla, y_na, gate, mod, lw, upd_tiles, n_lat_tiles,
                                tiles_per_batch, n_batch)
    return x_all[:rows_lat].reshape(n_batch, seq, d)
```

```python
import functools
import math

import numpy as np
import jax
import jax.numpy as jnp
from jax import lax
from jax.experimental import pallas as pl
from jax.experimental.pallas import tpu as pltpu

D_MODEL = 1024
GRID_W = 64
MLA_HEADS = 8
MLA_NOPE = 64
MLA_ROPE = 32
MLA_V = 64
MLA_QK = MLA_NOPE + MLA_ROPE
Q_LORA = 768
KV_LORA = 256
NA_HEADS = 8
NA_DIM = 64
NA_KH = 8
NA_KW = 16
N_BRANCH = 2
ROPE_BASE = 10000.0
ROPE_PAIRS = MLA_ROPE // 4
EPS = 1e-6
OFF_CQ = N_BRANCH * D_MODEL
OFF_CKV = OFF_CQ + Q_LORA
OFF_KR = OFF_CKV + KV_LORA
OFF_NA = OFF_KR + MLA_ROPE

LANES = 128
HEAD_LANES = LANES // 2
assert MLA_V == HEAD_LANES and NA_DIM == HEAD_LANES
ROW_TILE = 256
NA_BAND = 3 * ROW_TILE
TILES_PER_STEP = 2
SCORE_LOOKAHEAD = 4
NEG_BIG = -1e30
LOG2E = math.log2(math.e)
SHIFT_LIMIT = 40.0
VMEM_LIMIT = 56 * 1024 * 1024

P_GATE = 0
P_CQ = OFF_CQ
P_CKV = P_CQ + Q_LORA
P_KR = P_CKV + KV_LORA
P_NA = P_KR + LANES
P_COLS = P_NA + 3 * NA_HEADS * NA_DIM

BF = jnp.bfloat16
F32 = jnp.float32


def _dot(a, b):
    return jnp.dot(a, b, preferred_element_type=F32)


def _dot_nt(a, b):
    return lax.dot_general(a, b, (((1,), (1,)), ((), ())), preferred_element_type=F32)


def _dot_tn(a, b):
    return lax.dot_general(a, b, (((0,), (0,)), ((), ())), preferred_element_type=F32)


def _rope_swap_local():
    q = MLA_ROPE // 4
    rope = np.arange(MLA_ROPE)
    swapped = np.concatenate([rope[q:2 * q], rope[:q], rope[3 * q:], rope[2 * q:3 * q]])
    return np.concatenate([np.arange(MLA_NOPE), MLA_NOPE + rope, MLA_NOPE + swapped])


def _single(block_shape, index_map):
    return pl.BlockSpec(block_shape, index_map, pipeline_mode=pl.Buffered(1))


def _layer(a, l):
    zeros = (0,) * (a.ndim - 1)
    return _single((None,) + a.shape[1:], lambda *_: (l,) + zeros)


def _ada_kernel(c_ref, w_ref, b_ref, o_ref):
    c = c_ref[...]
    s = (c * jax.nn.sigmoid(c)).astype(BF)
    o_ref[...] = _dot(s, w_ref[...].astype(BF)) + b_ref[...]


def _ada_call(cc, w_ada, b_ada):
    depth = w_ada.shape[0]
    nrow = cc.shape[0]
    ncol = w_ada.shape[2]
    bn = 1024
    return pl.pallas_call(
        _ada_kernel,
        grid=(depth, ncol // bn),
        in_specs=[
            pl.BlockSpec((nrow, D_MODEL), lambda l, j: (0, 0)),
            pl.BlockSpec((None, D_MODEL, bn), lambda l, j: (l, 0, j)),
            pl.BlockSpec((None, 1, bn), lambda l, j: (l, 0, j)),
        ],
        out_specs=pl.BlockSpec((None, nrow, bn), lambda l, j: (l, 0, j)),
        out_shape=jax.ShapeDtypeStruct((depth, nrow, ncol), F32),
        compiler_params=pltpu.CompilerParams(
            dimension_semantics=("arbitrary", "arbitrary"), vmem_limit_bytes=VMEM_LIMIT),
        name="ada_mod",
    )(cc, w_ada, b_ada.reshape(depth, 1, ncol))


def _rms(x, g):
    ms = jnp.mean(x * x, axis=-1, keepdims=True)
    return x * lax.rsqrt(ms + EPS) * g


def _proj_kernel(x_ref, sh_ref, sc_ref, g_attn_ref, w_in_ref, g_qa_ref, w_uq_ref, g_kva_ref,
                 w_ukv_ref, g_nq_ref, g_nk_ref, qpad_ref, kpad_ref, qcos_ref, qsin_ref, kcos_ref,
                 ksin_ref,
                 gate_ref, mq_ref, mk_ref, mv_ref, nq_ref, nk_ref, nv_ref):
    x = x_ref[...]
    h = _rms(x, g_attn_ref[...] * (1.0 + sc_ref[...])) + sh_ref[...]
    hb = h.astype(BF)

    lane = lax.broadcasted_iota(jnp.int32, (1, LANES), 1)
    qk_mean = jnp.where(lane < MLA_QK, 1.0 / MLA_QK, 0.0).astype(F32)

    def head_norm_rope(blk, cos, sin, pad):
        ms = jnp.sum(blk * blk * qk_mean, axis=-1, keepdims=True)
        y = blk * cos + pltpu.roll(blk, LANES - MLA_ROPE, 1) * sin
        return (y * lax.rsqrt(ms + EPS) + pad).astype(BF)

    cq = _dot(hb, w_in_ref[:, P_CQ:P_CKV])
    mq = _dot(_rms(cq, g_qa_ref[...]).astype(BF), w_uq_ref[...])
    qcos = qcos_ref[...]
    qsin = qsin_ref[...]
    for hd in range(MLA_HEADS):
        sl = slice(hd * LANES, (hd + 1) * LANES)
        mq_ref[:, sl] = head_norm_rope(mq[:, sl], qcos, qsin, qpad_ref[...])

    ckv = _dot(hb, w_in_ref[:, P_CKV:P_KR])
    ckvn = _rms(ckv, g_kva_ref[...]).astype(BF)
    kr = _dot(hb, w_in_ref[:, P_KR:P_NA])
    kk = _dot(ckvn, w_ukv_ref[:, :MLA_HEADS * LANES])
    mv_ref[...] = _dot(ckvn, w_ukv_ref[:, MLA_HEADS * LANES:]).astype(BF)
    kcos = kcos_ref[...]
    ksin = ksin_ref[...]
    for hd in range(MLA_HEADS):
        sl = slice(hd * LANES, (hd + 1) * LANES)
        mk_ref[:, sl] = head_norm_rope(kk[:, sl] + kr, kcos, ksin, kpad_ref[...])

    na = _dot(hb, w_in_ref[:, P_NA:P_COLS])
    width = NA_HEADS * NA_DIM
    is_lo = lane < NA_DIM
    lo_mean = jnp.where(is_lo, 1.0 / NA_DIM, 0.0).astype(F32)
    hi_mean = jnp.where(is_lo, 0.0, 1.0 / NA_DIM).astype(F32)

    def pair_norm(blk, g):
        sq = blk * blk
        ms_lo = jnp.sum(sq * lo_mean, axis=-1, keepdims=True)
        ms_hi = jnp.sum(sq * hi_mean, axis=-1, keepdims=True)
        r = lax.rsqrt(jnp.where(is_lo, ms_lo, ms_hi) + EPS)
        return (blk * g * r).astype(BF)

    g_nq = g_nq_ref[...]
    g_nk = g_nk_ref[...]
    for pr in range(width // LANES):
        sl = slice(pr * LANES, (pr + 1) * LANES)
        nq_ref[:, sl] = pair_norm(na[:, sl], g_nq)
        nk_ref[:, sl] = pair_norm(na[:, width + pr * LANES: width + (pr + 1) * LANES], g_nk)
    nv_ref[...] = na[:, 2 * width:].astype(BF)

    gate_chunk = 512
    for c0 in range(P_GATE, P_CQ, gate_chunk):
        logits = _dot(hb, w_in_ref[:, c0:c0 + gate_chunk])
        gate_ref[:, c0:c0 + gate_chunk] = jax.nn.sigmoid(logits).astype(BF)


def _proj_call(x_all, mod, sw, l, n_lat_tiles, tiles_per_batch, n_batch):
    rows = x_all.shape[0]
    n_tiles = rows // ROW_TILE
    tm = ROW_TILE

    def mod_row(i):
        return jnp.where(i < n_lat_tiles, i // tiles_per_batch, n_batch)

    def rope_blk(i):
        return jnp.where(i < n_lat_tiles, i % tiles_per_batch, tiles_per_batch)

    names = ["g_attn", "w_in", "g_qa", "w_uq", "g_kva", "w_ukv", "g_nq", "g_nk", "q_pad", "k_pad"]
    rope = pl.BlockSpec((None, tm, LANES), lambda i: (l, rope_blk(i), 0))
    in_specs = [
        pl.BlockSpec((tm, D_MODEL), lambda i: (i, 0)),
        pl.BlockSpec((None, None, 1, D_MODEL), lambda i: (l, mod_row(i), 0, 0)),
        pl.BlockSpec((None, None, 1, D_MODEL), lambda i: (l, mod_row(i), 0, 1)),
        *[_layer(sw[n], l) for n in names],
        rope, rope, rope, rope,
    ]
    widths = [N_BRANCH * D_MODEL, MLA_HEADS * LANES, MLA_HEADS * LANES, MLA_HEADS * MLA_V,
              NA_HEADS * NA_DIM, NA_HEADS * NA_DIM, NA_HEADS * NA_DIM]
    out_specs = [pl.BlockSpec((tm, w), lambda i: (i, 0)) for w in widths]
    out_shape = [jax.ShapeDtypeStruct((rows, w), BF) for w in widths]
    return pl.pallas_call(
        _proj_kernel,
        grid=(n_tiles,),
        in_specs=in_specs,
        out_specs=out_specs,
        out_shape=out_shape,
        compiler_params=pltpu.CompilerParams(
            dimension_semantics=("arbitrary",), vmem_limit_bytes=VMEM_LIMIT),
        name="mixer_proj",
    )(x_all, mod, mod, *[sw[n] for n in names], sw["qcos"], sw["qsin"], sw["kcos"], sw["ksin"])


def _flash_transposed(qs, chunk_lists, s_scr, p_scr, bounded):
    n_streams = len(qs)
    n_chunks = len(chunk_lists[0])
    accs = [None] * n_streams

    def chunk_scores(i, c):
        k, _, bias, shift = chunk_lists[i][c]
        s = _dot_nt(k(), qs[i])
        if bias is not None:
            s = s + bias()
        if shift is not None:
            s = s - shift
        return s

    def chunk_values(i, c):
        return chunk_lists[i][c][1]()

    if bounded:
        order = [(i, c) for c in range(n_chunks) for i in range(n_streams)]
        pending = []
        for n in range(len(order) + SCORE_LOOKAHEAD):
            if n < len(order):
                pending.append(chunk_scores(*order[n]))
            if n >= SCORE_LOOKAHEAD:
                i, c = order[n - SCORE_LOOKAHEAD]
                part = _dot(chunk_values(i, c), jnp.exp2(pending.pop(0)).astype(BF))
                accs[i] = part if accs[i] is None else accs[i] + part
        return accs

    maxes = [None] * n_streams

    def stage_scores(i):
        piece_max = []
        for c in range(n_chunks):
            s = chunk_scores(i, c)
            s_scr[i, c * ROW_TILE:(c + 1) * ROW_TILE, :] = s
            piece_max.append(jnp.max(s, axis=0, keepdims=True))
        maxes[i] = functools.reduce(jnp.maximum, piece_max)

    def stage_softmax(i):
        n_keys = n_chunks * ROW_TILE
        p_scr[i, :n_keys, :] = jnp.exp2((s_scr[i, :n_keys, :] - maxes[i]).astype(BF))

    def stage_values(i):
        parts = [_dot(chunk_values(i, c), p_scr[i, c * ROW_TILE:(c + 1) * ROW_TILE, :])
                 for c in range(n_chunks)]
        accs[i] = functools.reduce(lambda a, b: a + b, parts)

    for t in range(n_streams + 2):
        if t < n_streams:
            stage_scores(t)
        if 0 <= t - 1 < n_streams:
            stage_softmax(t - 1)
        if 0 <= t - 2 < n_streams:
            stage_values(t - 2)
    return accs


def _ones_outside_head(v, hl):
    lane = lax.broadcasted_iota(jnp.int32, (1, LANES), 1)
    own = (lane >= hl * HEAD_LANES) & (lane < (hl + 1) * HEAD_LANES)
    return jnp.where(own, v, jnp.ones_like(v))


def _normalise_pair(acc0, acc1):
    num = jnp.concatenate([acc0[:HEAD_LANES], acc1[HEAD_LANES:]], axis=0)
    den = jnp.concatenate([acc0[HEAD_LANES:], acc1[:HEAD_LANES]], axis=0)
    return (num / den).T


def _augment_values(vl_ref, vc_ref, vl_aug, vc_aug):
    for hl in range(2):
        for t in range(vl_ref.shape[0] // ROW_TILE):
            v = vl_ref[t * ROW_TILE:(t + 1) * ROW_TILE, :]
            vl_aug[hl, t] = _ones_outside_head(v, hl).astype(F32).T.astype(BF)
        vc_aug[hl] = _ones_outside_head(vc_ref[...], hl).astype(F32).T.astype(BF)


def _mla_kernel(ql_ref, qc_ref, kl_ref, kc_ref, vl_ref, vc_ref, ol_ref, oc_ref, vl_aug, vc_aug,
                s_scr, p_scr, *, bounded):
    seq = kl_ref.shape[0]
    _augment_values(vl_ref, vc_ref, vl_aug, vc_aug)

    def tiles(q_tiles, with_latent):
        qs, chunk_lists = [], []
        for q in q_tiles:
            for hl in range(2):
                sl = slice(hl * LANES, (hl + 1) * LANES)
                chunks = [(lambda sl=sl: kc_ref[:, sl], lambda hl=hl: vc_aug[hl], None, None)]
                for t in range(seq // ROW_TILE if with_latent else 0):
                    rows = slice(t * ROW_TILE, (t + 1) * ROW_TILE)
                    chunks.append((lambda rows=rows, sl=sl: kl_ref[rows, sl],
                                   lambda hl=hl, t=t: vl_aug[hl, t], None, None))
                qs.append(q[:, sl])
                chunk_lists.append(chunks)
        accs = _flash_transposed(qs, chunk_lists, s_scr, p_scr, bounded)
        return [_normalise_pair(accs[2 * i], accs[2 * i + 1]) for i in range(len(q_tiles))]

    def body(t, carry):
        rows = [pl.ds(pl.multiple_of((TILES_PER_STEP * t + i) * ROW_TILE, ROW_TILE), ROW_TILE)
                for i in range(TILES_PER_STEP)]
        outs = tiles([ql_ref[r, :] for r in rows], True)
        for r, o in zip(rows, outs):
            ol_ref[r, :] = o.astype(ol_ref.dtype)
        return carry

    lax.fori_loop(0, seq // (TILES_PER_STEP * ROW_TILE), body, 0)
    oc_ref[...] = tiles([qc_ref[...]], False)[0].astype(oc_ref.dtype)


def _attention_specs(n_batch, tiles_per_batch, q_lanes, batch_axis, max_keys):
    tq = ROW_TILE
    seq = tiles_per_batch * tq
    n_lat_tiles = n_batch * tiles_per_batch

    def lat(width):
        return pl.BlockSpec((seq, width), lambda *g: (g[batch_axis], g[1 - batch_axis]))

    def ctx(width):
        return pl.BlockSpec((tq, width),
                            lambda *g: (n_lat_tiles + g[batch_axis], g[1 - batch_axis]))

    in_specs = [lat(q_lanes), ctx(q_lanes), lat(q_lanes), ctx(q_lanes), lat(LANES), ctx(LANES)]
    out_specs = [lat(LANES),
                 pl.BlockSpec((tq, LANES), lambda *g: (g[batch_axis], g[1 - batch_axis]))]
    n_streams = 2 * TILES_PER_STEP
    scratch = [pltpu.VMEM((2, tiles_per_batch, LANES, tq), BF), pltpu.VMEM((2, LANES, tq), BF),
               pltpu.VMEM((n_streams, max_keys, tq), F32), pltpu.VMEM((n_streams, max_keys, tq), BF)]
    return in_specs, out_specs, scratch


def _mla_call(mq, mk, mv, n_batch, tiles_per_batch, bounded):
    seq = tiles_per_batch * ROW_TILE
    width = MLA_HEADS * MLA_V
    in_specs, out_specs, scratch = _attention_specs(
        n_batch, tiles_per_batch, 2 * LANES, 0, seq + ROW_TILE)
    return pl.pallas_call(
        functools.partial(_mla_kernel, bounded=bounded),
        grid=(n_batch, MLA_HEADS // 2),
        in_specs=in_specs,
        out_specs=out_specs,
        out_shape=[jax.ShapeDtypeStruct((n_batch * seq, width), BF),
                   jax.ShapeDtypeStruct((n_batch * ROW_TILE, width), BF)],
        scratch_shapes=scratch,
        compiler_params=pltpu.CompilerParams(
            dimension_semantics=("arbitrary", "arbitrary"), vmem_limit_bytes=VMEM_LIMIT),
        name="mla_attention",
    )(mq, mq, mk, mk, mv, mv)


def _na_kernel(shift_ref, ql_ref, qc_ref, kl_ref, kc_ref, vl_ref, vc_ref, bias_ref, ol_ref, oc_ref,
               vl_aug, vc_aug, s_scr, p_scr, *, layer, bounded):
    n_lat_q = kl_ref.shape[0] // ROW_TILE
    lane = lax.broadcasted_iota(jnp.int32, (1, LANES), 1)
    shift = shift_ref[layer]
    _augment_values(vl_ref, vc_ref, vl_aug, vc_aug)

    def tiles(q_tiles, js):
        qs, chunk_lists = [], []
        for q, j in zip(q_tiles, js):
            if j is not None:
                first = jnp.clip(j - 1, 0, n_lat_q - 3)
                pat = jnp.where(j == 0, 0, jnp.where(j == n_lat_q - 1, 2, 1))
            for hl in range(2):
                in_head = (lane >= hl * HEAD_LANES) & (lane < (hl + 1) * HEAD_LANES)
                chunks = [(lambda: kc_ref[...], lambda hl=hl: vc_aug[hl], None, shift)]
                for c in range(NA_BAND // ROW_TILE if j is not None else 0):
                    rows = slice(c * ROW_TILE, (c + 1) * ROW_TILE)

                    def k_chunk(first=first, c=c):
                        start = pl.multiple_of((first + c) * ROW_TILE, ROW_TILE)
                        return kl_ref[pl.ds(start, ROW_TILE), :]

                    chunks.append((k_chunk,
                                   lambda hl=hl, first=first, c=c: vl_aug[hl, first + c],
                                   lambda pat=pat, hl=hl, rows=rows: bias_ref[pat, hl, rows, :],
                                   None))
                qs.append(jnp.where(in_head, q, jnp.zeros_like(q)))
                chunk_lists.append(chunks)
        accs = _flash_transposed(qs, chunk_lists, s_scr, p_scr, bounded)
        return [_normalise_pair(accs[2 * i], accs[2 * i + 1]) for i in range(len(q_tiles))]

    def body(t, carry):
        js = [TILES_PER_STEP * t + i for i in range(TILES_PER_STEP)]
        rows = [pl.ds(pl.multiple_of(j * ROW_TILE, ROW_TILE), ROW_TILE) for j in js]
        outs = tiles([ql_ref[r, :] for r in rows], js)
        for r, o in zip(rows, outs):
            ol_ref[r, :] = o.astype(ol_ref.dtype)
        return carry

    lax.fori_loop(0, n_lat_q // TILES_PER_STEP, body, 0)
    oc_ref[...] = tiles([qc_ref[...]], [None])[0].astype(oc_ref.dtype)


def _na_call(nq, nk, nv, bias, shifts, l, n_batch, tiles_per_batch, bounded):
    seq = tiles_per_batch * ROW_TILE
    width = NA_HEADS * NA_DIM
    in_specs, out_specs, scratch = _attention_specs(
        n_batch, tiles_per_batch, LANES, 1, NA_BAND + ROW_TILE)
    in_specs.insert(0, pl.BlockSpec(memory_space=pltpu.SMEM))
    in_specs.append(pl.BlockSpec((None, 3, 2, NA_BAND, ROW_TILE), lambda hp, b: (l, 0, hp, 0, 0)))
    return pl.pallas_call(
        functools.partial(_na_kernel, layer=l, bounded=bounded),
        grid=(NA_HEADS // 2, n_batch),
        in_specs=in_specs,
        out_specs=out_specs,
        out_shape=[jax.ShapeDtypeStruct((n_batch * seq, width), BF),
                   jax.ShapeDtypeStruct((n_batch * ROW_TILE, width), BF)],
        scratch_shapes=scratch,
        compiler_params=pltpu.CompilerParams(
            dimension_semantics=("arbitrary", "arbitrary"), vmem_limit_bytes=VMEM_LIMIT),
        name="na_attention",
    )(shifts, nq, nq, nk, nk, nv, nv, bias)


def _na_window_plan(rows):
    rows_per_tile = ROW_TILE // GRID_W
    n_tiles = rows // rows_per_tile
    band_rows = NA_BAND // GRID_W
    kh = min(NA_KH, rows)
    assert band_rows >= kh + rows_per_tile - 1 and rows_per_tile % 2 == 0
    plan = []
    for j in (0, 1, n_tiles - 1):
        band_start = int(np.clip(j - 1, 0, n_tiles - 3)) * rows_per_tile
        tile_plan = []
        for k_r in range(band_rows):
            key_row = band_start + k_r
            for qp in range(rows_per_tile // 2):
                ok = []
                for r in (j * rows_per_tile + 2 * qp, j * rows_per_tile + 2 * qp + 1):
                    r_start = int(np.clip(r - kh // 2, 0, rows - kh))
                    ok.append(r_start <= key_row < r_start + kh)
                d_left = key_row - (j * rows_per_tile + 2 * qp) + (NA_KH - 1)
                tile_plan.append((k_r, qp, int(np.clip(d_left, 0, 2 * NA_KH - 1)), ok[0], ok[1]))
        plan.append(tile_plan)
    return plan


def _na_bias_kernel(tab_ref, o_ref, *, plan):
    lane = lax.broadcasted_iota(jnp.int32, (1, 2 * GRID_W), 1)
    neg = jnp.full((GRID_W, 2 * GRID_W), NEG_BIG, F32)
    for pi, tile_plan in enumerate(plan):
        for k_r, qp, d_left, left_ok, right_ok in tile_plan:
            if left_ok and right_ok:
                blk = tab_ref[d_left]
            elif left_ok:
                blk = jnp.where(lane < GRID_W, tab_ref[d_left], NEG_BIG)
            elif right_ok:
                blk = jnp.where(lane < GRID_W, NEG_BIG, tab_ref[d_left])
            else:
                blk = neg
            o_ref[pi, k_r * GRID_W:(k_r + 1) * GRID_W, qp * 2 * GRID_W:(qp + 1) * 2 * GRID_W] = blk


def _na_bias_tables(rpb, rows, shifts):
    depth, heads = rpb.shape[:2]
    n_dr = 2 * NA_KH - 1
    k_c = np.arange(GRID_W)[:, None]
    q_c = np.arange(GRID_W)[None, :]
    c_start = np.clip(q_c - NA_KW // 2, 0, GRID_W - NA_KW)
    col_ok = (k_c >= c_start) & (k_c < c_start + NA_KW)
    dc = k_c - q_c + (NA_KW - 1)
    onehot = (np.arange(2 * NA_KW - 1)[:, None, None] == dc[None]).astype(np.float32)
    picked = jnp.einsum("lhdc,ckq->lhdkq", rpb, jnp.asarray(onehot),
                        precision=lax.Precision.HIGHEST)
    shifted = picked * LOG2E - shifts[:, None, None, None, None]
    col_tab = jnp.where(jnp.asarray(col_ok), shifted, NEG_BIG)
    neg = jnp.full((depth, heads, 1, GRID_W, GRID_W), NEG_BIG, F32)
    pair_tab = jnp.concatenate([jnp.concatenate([col_tab, neg], axis=2),
                                jnp.concatenate([neg, col_tab], axis=2)], axis=-1)
    plan = _na_window_plan(rows)
    return pl.pallas_call(
        functools.partial(_na_bias_kernel, plan=plan),
        grid=(depth, heads),
        in_specs=[pl.BlockSpec((None, None, n_dr + 1, GRID_W, 2 * GRID_W),
                               lambda l, h: (l, h, 0, 0, 0))],
        out_specs=pl.BlockSpec((None, 3, None, NA_BAND, ROW_TILE), lambda l, h: (l, 0, h, 0, 0)),
        out_shape=jax.ShapeDtypeStruct((depth, 3, heads, NA_BAND, ROW_TILE), F32),
        compiler_params=pltpu.CompilerParams(
            dimension_semantics=("arbitrary", "arbitrary"), vmem_limit_bytes=VMEM_LIMIT),
        name="na_bias_tables",
    )(pair_tab)


def _merge_mlp_kernel(x_ref, yml_ref, ymc_ref, ynl_ref, ync_ref, gate_ref, gt1_ref, sh_ref, sc_ref,
                      gt2_ref, w_mo_ref, w_no_ref, w_out_ref, g_ref, w1_ref, w2_ref, o_ref, *,
                      n_lat_tiles):
    is_lat = pl.program_id(0) < n_lat_tiles
    a = _dot(jnp.where(is_lat, yml_ref[...], ymc_ref[...]), w_mo_ref[...])
    b = _dot(jnp.where(is_lat, ynl_ref[...], ync_ref[...]), w_no_ref[...])
    y = gate_ref[:, :D_MODEL].astype(F32) * a + gate_ref[:, D_MODEL:].astype(F32) * b
    x = x_ref[...] + gt1_ref[...] * _dot(y.astype(BF), w_out_ref[...])

    h = (_rms(x, g_ref[...] * (1.0 + sc_ref[...])) + sh_ref[...]).astype(BF)
    d_ff = w1_ref.shape[1]
    chunk = 1024
    acc = None
    for c0 in range(0, d_ff, chunk):
        u = jnp.maximum(_dot(h, w1_ref[:, c0:c0 + chunk]), 0.0)
        part = _dot((u * u).astype(BF), w2_ref[c0:c0 + chunk, :])
        acc = part if acc is None else acc + part
    o_ref[...] = x + gt2_ref[...] * acc


def _merge_mlp_call(x_all, y_mla, y_na, gate, mod, sw, l, n_tiles, n_lat_tiles, tiles_per_batch,
                    n_batch):
    tm = ROW_TILE
    mod_row = lambda i: jnp.where(i < n_lat_tiles, i // tiles_per_batch, n_batch)
    mod_spec = lambda col: pl.BlockSpec((None, None, 1, D_MODEL),
                                        lambda i: (l, mod_row(i), 0, col))
    row = lambda w: pl.BlockSpec((tm, w), lambda i: (i, 0))
    lat = lambda w: pl.BlockSpec((tm, w), lambda i: (jnp.minimum(i, n_lat_tiles - 1), 0))
    ctx = lambda w: pl.BlockSpec((tm, w), lambda i: (jnp.maximum(i - n_lat_tiles, 0), 0))
    names = ["w_mla_o", "w_na_o", "w_out", "g_mlp", "w_ff1", "w_ff2"]
    return pl.pallas_call(
        functools.partial(_merge_mlp_kernel, n_lat_tiles=n_lat_tiles),
        grid=(n_tiles,),
        in_specs=[row(D_MODEL), lat(y_mla[0].shape[1]), ctx(y_mla[1].shape[1]),
                  lat(y_na[0].shape[1]), ctx(y_na[1].shape[1]), row(gate.shape[1]),
                  mod_spec(2), mod_spec(3), mod_spec(4), mod_spec(5),
                  *[_layer(sw[n], l) for n in names]],
        out_specs=row(D_MODEL),
        out_shape=jax.ShapeDtypeStruct((n_tiles * tm, D_MODEL), F32),
        compiler_params=pltpu.CompilerParams(
            dimension_semantics=("arbitrary",), vmem_limit_bytes=VMEM_LIMIT),
        name="merge_mlp",
    )(x_all, y_mla[0], y_mla[1], y_na[0], y_na[1], gate, mod, mod, mod, mod,
      *[sw[n] for n in names])


def _take_cols(w, idx):
    idx = np.asarray(idx)
    pieces = []
    start = 0
    while start < len(idx):
        stop = start + 1
        if idx[start] < 0:
            while stop < len(idx) and idx[stop] < 0:
                stop += 1
            pieces.append(jnp.zeros(w.shape[:-1] + (stop - start,), w.dtype))
        else:
            while stop < len(idx) and idx[stop] == idx[stop - 1] + 1:
                stop += 1
            pieces.append(w[..., int(idx[start]):int(idx[stop - 1]) + 1])
        start = stop
    return jnp.concatenate(pieces, axis=-1)


def _pack_weights(w_in, w_uq, w_ukv, g_mla_q, g_mla_k, g_na_q, g_na_k, seq):
    local = _rope_swap_local()
    kr_local = local[MLA_NOPE:] - MLA_NOPE
    in_idx = np.concatenate([
        np.arange(OFF_KR),
        np.full(MLA_NOPE, -1), OFF_KR + kr_local,
        np.arange(OFF_NA, OFF_NA + 3 * NA_HEADS * NA_DIM)])
    uq_idx = np.concatenate([hd * MLA_QK + local for hd in range(MLA_HEADS)])
    per_head = MLA_NOPE + MLA_V
    ukv_k = np.concatenate([np.concatenate([hd * per_head + np.arange(MLA_NOPE),
                                            np.full(LANES - MLA_NOPE, -1)])
                            for hd in range(MLA_HEADS)])
    ukv_v = np.concatenate([hd * per_head + MLA_NOPE + np.arange(MLA_V) for hd in range(MLA_HEADS)])
    pair = np.concatenate([np.arange(NA_DIM), np.arange(NA_DIM)])

    cos_base, sin_base = _rope_tables(seq)
    partner = (np.arange(LANES) + MLA_ROPE) % LANES

    def rope_tabs(g, scale):
        g_ext = _take_cols(g, local)
        g_partner = _take_cols(g, local[partner])
        return (cos_base[None] * (g_ext * scale)[:, None, :],
                sin_base[None] * (g_partner * scale)[:, None, :])

    qcos, qsin = rope_tabs(g_mla_q, LOG2E / math.sqrt(MLA_QK))
    kcos, ksin = rope_tabs(g_mla_k, 1.0)
    return {
        "w_in": _take_cols(w_in.astype(BF), in_idx),
        "w_uq": _take_cols(w_uq.astype(BF), uq_idx),
        "w_ukv": _take_cols(w_ukv.astype(BF), np.concatenate([ukv_k, ukv_v])),
        "qcos": qcos, "qsin": qsin, "kcos": kcos, "ksin": ksin,
        "g_nq": (_take_cols(g_na_q, pair) * (LOG2E / math.sqrt(NA_DIM)))[:, None, :],
        "g_nk": _take_cols(g_na_k, pair)[:, None, :],
    }


def _rope_tables(seq):
    t = np.arange(seq)
    inv_freq = jnp.asarray(ROPE_BASE, F32) ** (-jnp.arange(ROPE_PAIRS, dtype=F32) / ROPE_PAIRS)
    ang_r = jnp.asarray(t // GRID_W, F32)[:, None] * inv_freq
    ang_c = jnp.asarray(t % GRID_W, F32)[:, None] * inv_freq
    cr, sr, cc, sn = jnp.cos(ang_r), jnp.sin(ang_r), jnp.cos(ang_c), jnp.sin(ang_c)
    ones = jnp.ones((seq, MLA_NOPE), F32)
    zeros = jnp.zeros((seq, MLA_NOPE), F32)
    pad = jnp.zeros((seq, LANES - MLA_QK), F32)
    cos_lat = jnp.concatenate([ones, cr, cr, cc, cc, pad], axis=1)
    sin_lat = jnp.concatenate([zeros, -sr, sr, -sn, sn, pad], axis=1)
    cos_id = jnp.concatenate([jnp.ones((ROW_TILE, MLA_QK), F32),
                              jnp.zeros((ROW_TILE, LANES - MLA_QK), F32)], axis=1)
    sin_id = jnp.zeros((ROW_TILE, LANES), F32)
    return jnp.concatenate([cos_lat, cos_id], axis=0), jnp.concatenate([sin_lat, sin_id], axis=0)


def kernel(x, c, ctx, c_ctx, w_ada, b_ada, g_attn, w_in, g_qa, w_uq, g_kva, w_ukv, g_mla_q,
           g_mla_k, g_na_q, g_na_k, rpb, w_mla_o, w_na_o, w_out, g_mlp, w_ff1, w_ff2):
    n_batch, seq, d = x.shape
    ctx_len = ctx.shape[1]
    depth = w_ada.shape[0]
    assert d == D_MODEL and ctx_len == ROW_TILE and seq % (TILES_PER_STEP * ROW_TILE) == 0
    tiles_per_batch = seq // ROW_TILE
    assert tiles_per_batch >= 3
    n_lat_tiles = n_batch * tiles_per_batch
    n_tiles = n_lat_tiles + n_batch
    rows_lat = n_batch * seq

    pad_rows = (-(n_batch + 1)) % 8
    cc = jnp.concatenate([c, c_ctx[None, :], jnp.zeros((pad_rows, d), F32)], axis=0)
    mod_all = _ada_call(cc, w_ada, b_ada)

    sw = _pack_weights(w_in, w_uq, w_ukv, g_mla_q, g_mla_k, g_na_q, g_na_k, seq)
    sw.update({
        "g_attn": g_attn[:, None, :], "g_qa": g_qa[:, None, :], "g_kva": g_kva[:, None, :],
        "g_mlp": g_mlp[:, None, :],
        "w_mla_o": w_mla_o.astype(BF), "w_na_o": w_na_o.astype(BF), "w_out": w_out.astype(BF),
        "w_ff1": w_ff1.astype(BF), "w_ff2": w_ff2.astype(BF),
    })
    gmax = lambda g: jnp.max(jnp.abs(g), axis=1)
    slack = 1.02
    mla_shift = slack * LOG2E * math.sqrt(MLA_QK) * gmax(g_mla_q) * gmax(g_mla_k)
    na_shift = (slack * LOG2E * math.sqrt(NA_DIM) * gmax(g_na_q) * gmax(g_na_k)
                + LOG2E * jnp.maximum(jnp.max(rpb, axis=(1, 2, 3)), 0.0))
    bounded_ok = jnp.maximum(jnp.max(mla_shift), jnp.max(na_shift)) <= SHIFT_LIMIT
    lane = np.arange(LANES)
    sw["q_pad"] = jnp.where(lane == MLA_QK, -mla_shift[:, None, None], 0.0).astype(F32)
    sw["k_pad"] = jnp.broadcast_to(jnp.asarray(lane == MLA_QK, F32), (depth, 1, LANES))

    bias_all = _na_bias_tables(rpb, seq // GRID_W, na_shift)
    mod = mod_all.reshape(depth, mod_all.shape[1], 1, 6 * d)
    x_all = jnp.concatenate([x.reshape(rows_lat, d), ctx.reshape(n_batch * ctx_len, d)], axis=0)

    def trunk(x_all, bounded):
        for l in range(depth):
            gate, mq, mk, mv, nq, nk, nv = _proj_call(
                x_all, mod, sw, l, n_lat_tiles, tiles_per_batch, n_batch)
            y_mla = _mla_call(mq, mk, mv, n_batch, tiles_per_batch, bounded)
            y_na = _na_call(nq, nk, nv, bias_all, na_shift, l, n_batch, tiles_per_batch, bounded)
            upd_tiles = n_lat_tiles if l == depth - 1 else n_tiles
            x_all = _merge_mlp_call(x_all, y_mla, y_na, gate, mod, sw, l, upd_tiles,
                                    n_lat_tiles, tiles_per_batch, n_batch)
        return x_all

    out = lax.cond(bounded_ok, functools.partial(trunk, bounded=True),
                   functools.partial(trunk, bounded=False), x_all)
    return out.reshape(n_batch, seq, d)
```

```python
import functools
import math

import numpy as np
import jax
import jax.numpy as jnp
from jax import lax
from jax.experimental import pallas as pl
from jax.experimental.pallas import tpu as pltpu

D_MODEL = 1024
GRID_W = 64
MLA_HEADS = 8
MLA_NOPE = 64
MLA_ROPE = 32
MLA_V = 64
MLA_QK = MLA_NOPE + MLA_ROPE
Q_LORA = 768
KV_LORA = 256
NA_HEADS = 8
NA_DIM = 64
NA_KH = 8
NA_KW = 16
N_BRANCH = 2
ROPE_BASE = 10000.0
ROPE_PAIRS = MLA_ROPE // 4
EPS = 1e-6
OFF_CQ = N_BRANCH * D_MODEL
OFF_CKV = OFF_CQ + Q_LORA
OFF_KR = OFF_CKV + KV_LORA
OFF_NA = OFF_KR + MLA_ROPE

LANES = 128
HEAD_LANES = LANES // 2
assert MLA_V == HEAD_LANES and NA_DIM == HEAD_LANES
ROW_TILE = 256
NA_BAND = 3 * ROW_TILE
TILES_PER_STEP = 2
SCORE_LOOKAHEAD = 4
NEG_BIG = -1e30
LOG2E = math.log2(math.e)
SHIFT_LIMIT = 40.0
VMEM_LIMIT = 56 * 1024 * 1024

P_GATE = 0
P_CQ = OFF_CQ
P_CKV = P_CQ + Q_LORA
P_KR = P_CKV + KV_LORA
P_NA = P_KR + LANES
P_COLS = P_NA + 3 * NA_HEADS * NA_DIM

BF = jnp.bfloat16
F32 = jnp.float32


def _dot(a, b):
    return jnp.dot(a, b, preferred_element_type=F32)


def _dot_nt(a, b):
    return lax.dot_general(a, b, (((1,), (1,)), ((), ())), preferred_element_type=F32)


def _dot_tn(a, b):
    return lax.dot_general(a, b, (((0,), (0,)), ((), ())), preferred_element_type=F32)


def _rope_swap_local():
    q = MLA_ROPE // 4
    rope = np.arange(MLA_ROPE)
    swapped = np.concatenate([rope[q:2 * q], rope[:q], rope[3 * q:], rope[2 * q:3 * q]])
    return np.concatenate([np.arange(MLA_NOPE), MLA_NOPE + rope, MLA_NOPE + swapped])


def _single(block_shape, index_map):
    return pl.BlockSpec(block_shape, index_map, pipeline_mode=pl.Buffered(1))


def _layer(a, l):
    zeros = (0,) * (a.ndim - 1)
    return _single((None,) + a.shape[1:], lambda *_: (l,) + zeros)


def _ada_kernel(c_ref, w_ref, b_ref, o_ref):
    c = c_ref[...]
    s = (c * jax.nn.sigmoid(c)).astype(BF)
    o_ref[...] = _dot(s, w_ref[...].astype(BF)) + b_ref[...]


def _ada_call(cc, w_ada, b_ada):
    depth = w_ada.shape[0]
    nrow = cc.shape[0]
    ncol = w_ada.shape[2]
    bn = 1024
    return pl.pallas_call(
        _ada_kernel,
        grid=(depth, ncol // bn),
        in_specs=[
            pl.BlockSpec((nrow, D_MODEL), lambda l, j: (0, 0)),
            pl.BlockSpec((None, D_MODEL, bn), lambda l, j: (l, 0, j)),
            pl.BlockSpec((None, 1, bn), lambda l, j: (l, 0, j)),
        ],
        out_specs=pl.BlockSpec((None, nrow, bn), lambda l, j: (l, 0, j)),
        out_shape=jax.ShapeDtypeStruct((depth, nrow, ncol), F32),
        compiler_params=pltpu.CompilerParams(
            dimension_semantics=("arbitrary", "arbitrary"), vmem_limit_bytes=VMEM_LIMIT),
        name="ada_mod",
    )(cc, w_ada, b_ada.reshape(depth, 1, ncol))


def _rms(x, g):
    ms = jnp.mean(x * x, axis=-1, keepdims=True)
    return x * lax.rsqrt(ms + EPS) * g


def _proj_kernel(xl_ref, xc_ref, sh_ref, sc_ref, g_attn_ref, w_in_ref, g_qa_ref, w_uq_ref,
                 g_kva_ref, w_ukv_ref, g_nq_ref, g_nk_ref, qpad_ref, kpad_ref, qcos_ref, qsin_ref,
                 kcos_ref, ksin_ref,
                 gate_ref, mq_ref, mk_ref, mv_ref, nq_ref, nk_ref, nv_ref, *, n_lat_tiles):
    x = jnp.where(pl.program_id(0) < n_lat_tiles, xl_ref[...], xc_ref[...])
    h = _rms(x, g_attn_ref[...] * (1.0 + sc_ref[...])) + sh_ref[...]
    hb = h.astype(BF)

    lane = lax.broadcasted_iota(jnp.int32, (1, LANES), 1)
    qk_mean = jnp.where(lane < MLA_QK, 1.0 / MLA_QK, 0.0).astype(F32)

    def head_norm_rope(blk, cos, sin, pad):
        ms = jnp.sum(blk * blk * qk_mean, axis=-1, keepdims=True)
        y = blk * cos + pltpu.roll(blk, LANES - MLA_ROPE, 1) * sin
        return (y * lax.rsqrt(ms + EPS) + pad).astype(BF)

    gate_chunk = (P_CQ - P_GATE) // 4

    def gate_logits(n):
        return _dot(hb, w_in_ref[:, P_GATE + n * gate_chunk:P_GATE + (n + 1) * gate_chunk])

    def store_gate(n, logits):
        gate_ref[:, n * gate_chunk:(n + 1) * gate_chunk] = jax.nn.sigmoid(logits).astype(BF)

    cq = _dot(hb, w_in_ref[:, P_CQ:P_CKV])
    ckv = _dot(hb, w_in_ref[:, P_CKV:P_KR])
    kr = _dot(hb, w_in_ref[:, P_KR:P_NA])
    gate0 = gate_logits(0)
    mq = _dot(_rms(cq, g_qa_ref[...]).astype(BF), w_uq_ref[...])
    ckvn = _rms(ckv, g_kva_ref[...]).astype(BF)
    kk = _dot(ckvn, w_ukv_ref[:, :MLA_HEADS * LANES])
    vv = _dot(ckvn, w_ukv_ref[:, MLA_HEADS * LANES:])
    gate1 = gate_logits(1)
    store_gate(0, gate0)

    qcos = qcos_ref[...]
    qsin = qsin_ref[...]
    for hd in range(MLA_HEADS):
        sl = slice(hd * LANES, (hd + 1) * LANES)
        mq_ref[:, sl] = head_norm_rope(mq[:, sl], qcos, qsin, qpad_ref[...])

    na = _dot(hb, w_in_ref[:, P_NA:P_COLS])

    mv_ref[...] = vv.astype(BF)
    kcos = kcos_ref[...]
    ksin = ksin_ref[...]
    for hd in range(MLA_HEADS):
        sl = slice(hd * LANES, (hd + 1) * LANES)
        mk_ref[:, sl] = head_norm_rope(kk[:, sl] + kr, kcos, ksin, kpad_ref[...])

    gate2 = gate_logits(2)
    store_gate(1, gate1)
    gate3 = gate_logits(3)

    width = NA_HEADS * NA_DIM
    is_lo = lane < NA_DIM
    lo_mean = jnp.where(is_lo, 1.0 / NA_DIM, 0.0).astype(F32)
    hi_mean = jnp.where(is_lo, 0.0, 1.0 / NA_DIM).astype(F32)

    def pair_norm(blk, g):
        sq = blk * blk
        ms_lo = jnp.sum(sq * lo_mean, axis=-1, keepdims=True)
        ms_hi = jnp.sum(sq * hi_mean, axis=-1, keepdims=True)
        r = lax.rsqrt(jnp.where(is_lo, ms_lo, ms_hi) + EPS)
        return (blk * g * r).astype(BF)

    g_nq = g_nq_ref[...]
    g_nk = g_nk_ref[...]
    for pr in range(width // LANES):
        sl = slice(pr * LANES, (pr + 1) * LANES)
        nq_ref[:, sl] = pair_norm(na[:, sl], g_nq)
        nk_ref[:, sl] = pair_norm(na[:, width + pr * LANES: width + (pr + 1) * LANES], g_nk)
    nv_ref[...] = na[:, 2 * width:].astype(BF)
    store_gate(2, gate2)
    store_gate(3, gate3)


def _x_specs(xs, n_lat_tiles):
    ctx_base = xs[2]
    return [pl.BlockSpec((ROW_TILE, D_MODEL), lambda i: (jnp.minimum(i, n_lat_tiles - 1), 0)),
            pl.BlockSpec((ROW_TILE, D_MODEL),
                         lambda i: (ctx_base + jnp.maximum(i - n_lat_tiles, 0), 0))]


def _proj_call(xs, n_tiles, mod, sw, l, n_lat_tiles, tiles_per_batch, n_batch):
    rows = n_tiles * ROW_TILE
    tm = ROW_TILE

    def mod_row(i):
        return jnp.where(i < n_lat_tiles, i // tiles_per_batch, n_batch)

    def rope_blk(i):
        return jnp.where(i < n_lat_tiles, i % tiles_per_batch, tiles_per_batch)

    names = ["g_attn", "w_in", "g_qa", "w_uq", "g_kva", "w_ukv", "g_nq", "g_nk", "q_pad", "k_pad"]
    rope = pl.BlockSpec((None, tm, LANES), lambda i: (l, rope_blk(i), 0))
    in_specs = [
        *_x_specs(xs, n_lat_tiles),
        pl.BlockSpec((None, None, 1, D_MODEL), lambda i: (l, mod_row(i), 0, 0)),
        pl.BlockSpec((None, None, 1, D_MODEL), lambda i: (l, mod_row(i), 0, 1)),
        *[_layer(sw[n], l) for n in names],
        rope, rope, rope, rope,
    ]
    widths = [N_BRANCH * D_MODEL, MLA_HEADS * LANES, MLA_HEADS * LANES, MLA_HEADS * MLA_V,
              NA_HEADS * NA_DIM, NA_HEADS * NA_DIM, NA_HEADS * NA_DIM]
    out_specs = [pl.BlockSpec((tm, w), lambda i: (i, 0)) for w in widths]
    out_shape = [jax.ShapeDtypeStruct((rows, w), BF) for w in widths]
    return pl.pallas_call(
        functools.partial(_proj_kernel, n_lat_tiles=n_lat_tiles),
        grid=(n_tiles,),
        in_specs=in_specs,
        out_specs=out_specs,
        out_shape=out_shape,
        compiler_params=pltpu.CompilerParams(
            dimension_semantics=("arbitrary",), vmem_limit_bytes=VMEM_LIMIT),
        name="mixer_proj",
    )(xs[0], xs[1], mod, mod, *[sw[n] for n in names], sw["qcos"], sw["qsin"], sw["kcos"],
      sw["ksin"])


def _flash_transposed(qs, chunk_lists, s_scr, p_scr, bounded):
    n_streams = len(qs)
    n_chunks = [len(chunks) for chunks in chunk_lists]
    accs = [None] * n_streams

    def chunk_scores(i, c):
        k, _, bias, shift = chunk_lists[i][c]
        s = _dot_nt(k(), qs[i])
        if bias is not None:
            s = s + bias()
        if shift is not None:
            s = s - shift
        return s

    def chunk_values(i, c):
        return chunk_lists[i][c][1]()

    if bounded:
        order = [(i, c) for c in range(max(n_chunks)) for i in range(n_streams)
                 if c < n_chunks[i]]
        pending = []
        for n in range(len(order) + SCORE_LOOKAHEAD):
            if n < len(order):
                pending.append(chunk_scores(*order[n]))
            if n >= SCORE_LOOKAHEAD:
                i, c = order[n - SCORE_LOOKAHEAD]
                part = _dot(chunk_values(i, c), jnp.exp2(pending.pop(0)).astype(BF))
                accs[i] = part if accs[i] is None else accs[i] + part
        return accs

    maxes = [None] * n_streams

    def stage_scores(i):
        piece_max = []
        for c in range(n_chunks[i]):
            s = chunk_scores(i, c)
            s_scr[i, c * ROW_TILE:(c + 1) * ROW_TILE, :] = s
            piece_max.append(jnp.max(s, axis=0, keepdims=True))
        maxes[i] = functools.reduce(jnp.maximum, piece_max)

    def stage_softmax(i):
        n_keys = n_chunks[i] * ROW_TILE
        p_scr[i, :n_keys, :] = jnp.exp2((s_scr[i, :n_keys, :] - maxes[i]).astype(BF))

    def stage_values(i):
        parts = [_dot(chunk_values(i, c), p_scr[i, c * ROW_TILE:(c + 1) * ROW_TILE, :])
                 for c in range(n_chunks[i])]
        accs[i] = functools.reduce(lambda a, b: a + b, parts)

    for t in range(n_streams + 2):
        if t < n_streams:
            stage_scores(t)
        if 0 <= t - 1 < n_streams:
            stage_softmax(t - 1)
        if 0 <= t - 2 < n_streams:
            stage_values(t - 2)
    return accs


def _ones_outside_head(v, hl):
    lane = lax.broadcasted_iota(jnp.int32, (1, LANES), 1)
    own = (lane >= hl * HEAD_LANES) & (lane < (hl + 1) * HEAD_LANES)
    return jnp.where(own, v, jnp.ones_like(v))


def _normalise_pair(acc0, acc1):
    num = jnp.concatenate([acc0[:HEAD_LANES], acc1[HEAD_LANES:]], axis=0)
    den = jnp.concatenate([acc0[HEAD_LANES:], acc1[:HEAD_LANES]], axis=0)
    return (num / den).T


def _augment_values(vl_ref, vc_ref, vl_aug, vc_aug):
    for hl in range(2):
        for t in range(vl_ref.shape[0] // ROW_TILE):
            v = vl_ref[t * ROW_TILE:(t + 1) * ROW_TILE, :]
            vl_aug[hl, t] = _ones_outside_head(v, hl).astype(F32).T.astype(BF)
        vc_aug[hl] = _ones_outside_head(vc_ref[...], hl).astype(F32).T.astype(BF)


def _mla_kernel(ql_ref, qc_ref, kl_ref, kc_ref, vl_ref, vc_ref, ol_ref, oc_ref, vl_aug, vc_aug,
                s_scr, p_scr, *, bounded):
    seq = kl_ref.shape[0]
    _augment_values(vl_ref, vc_ref, vl_aug, vc_aug)

    def tiles(q_tiles, latent_flags):
        qs, chunk_lists = [], []
        for q, with_latent in zip(q_tiles, latent_flags):
            for hl in range(2):
                sl = slice(hl * LANES, (hl + 1) * LANES)
                chunks = [(lambda sl=sl: kc_ref[:, sl], lambda hl=hl: vc_aug[hl], None, None)]
                for t in range(seq // ROW_TILE if with_latent else 0):
                    rows = slice(t * ROW_TILE, (t + 1) * ROW_TILE)
                    chunks.append((lambda rows=rows, sl=sl: kl_ref[rows, sl],
                                   lambda hl=hl, t=t: vl_aug[hl, t], None, None))
                qs.append(q[:, sl])
                chunk_lists.append(chunks)
        accs = _flash_transposed(qs, chunk_lists, s_scr, p_scr, bounded)
        return [_normalise_pair(accs[2 * i], accs[2 * i + 1]) for i in range(len(q_tiles))]

    def body(t, carry):
        rows = [pl.ds(pl.multiple_of((TILES_PER_STEP * t + i) * ROW_TILE, ROW_TILE), ROW_TILE)
                for i in range(TILES_PER_STEP)]
        outs = tiles([ql_ref[r, :] for r in rows], [True] * TILES_PER_STEP)
        for r, o in zip(rows, outs):
            ol_ref[r, :] = o.astype(ol_ref.dtype)
        return carry

    n_loop = seq // (TILES_PER_STEP * ROW_TILE) - 1
    lax.fori_loop(0, n_loop, body, 0)
    rows = [slice((n_loop * TILES_PER_STEP + i) * ROW_TILE,
                  (n_loop * TILES_PER_STEP + i + 1) * ROW_TILE) for i in range(TILES_PER_STEP)]
    outs = tiles([ql_ref[r, :] for r in rows] + [qc_ref[...]], [True] * TILES_PER_STEP + [False])
    for r, o in zip(rows, outs):
        ol_ref[r, :] = o.astype(ol_ref.dtype)
    oc_ref[...] = outs[-1].astype(oc_ref.dtype)


def _attention_specs(n_batch, tiles_per_batch, q_lanes, batch_axis, max_keys):
    tq = ROW_TILE
    seq = tiles_per_batch * tq
    n_lat_tiles = n_batch * tiles_per_batch

    def lat(width):
        return pl.BlockSpec((seq, width), lambda *g: (g[batch_axis], g[1 - batch_axis]))

    def ctx(width):
        return pl.BlockSpec((tq, width),
                            lambda *g: (n_lat_tiles + g[batch_axis], g[1 - batch_axis]))

    in_specs = [lat(q_lanes), ctx(q_lanes), lat(q_lanes), ctx(q_lanes), lat(LANES), ctx(LANES)]
    out_specs = [lat(LANES),
                 pl.BlockSpec((tq, LANES), lambda *g: (g[batch_axis], g[1 - batch_axis]))]
    n_streams = 2 * (TILES_PER_STEP + 1)
    scratch = [pltpu.VMEM((2, tiles_per_batch, LANES, tq), BF), pltpu.VMEM((2, LANES, tq), BF),
               pltpu.VMEM((n_streams, max_keys, tq), F32), pltpu.VMEM((n_streams, max_keys, tq), BF)]
    return in_specs, out_specs, scratch


def _mla_call(mq, mk, mv, n_batch, tiles_per_batch, bounded):
    seq = tiles_per_batch * ROW_TILE
    width = MLA_HEADS * MLA_V
    in_specs, out_specs, scratch = _attention_specs(
        n_batch, tiles_per_batch, 2 * LANES, 0, seq + ROW_TILE)
    return pl.pallas_call(
        functools.partial(_mla_kernel, bounded=bounded),
        grid=(n_batch, MLA_HEADS // 2),
        in_specs=in_specs,
        out_specs=out_specs,
        out_shape=[jax.ShapeDtypeStruct((n_batch * seq, width), BF),
                   jax.ShapeDtypeStruct((n_batch * ROW_TILE, width), BF)],
        scratch_shapes=scratch,
        compiler_params=pltpu.CompilerParams(
            dimension_semantics=("arbitrary", "arbitrary"), vmem_limit_bytes=VMEM_LIMIT),
        name="mla_attention",
    )(mq, mq, mk, mk, mv, mv)


def _na_kernel(shift_ref, ql_ref, qc_ref, kl_ref, kc_ref, vl_ref, vc_ref, bias_ref, ol_ref, oc_ref,
               vl_aug, vc_aug, s_scr, p_scr, *, layer, bounded):
    n_lat_q = kl_ref.shape[0] // ROW_TILE
    lane = lax.broadcasted_iota(jnp.int32, (1, LANES), 1)
    shift = shift_ref[layer]
    _augment_values(vl_ref, vc_ref, vl_aug, vc_aug)

    def tiles(q_tiles, js):
        qs, chunk_lists = [], []
        for q, j in zip(q_tiles, js):
            if isinstance(j, int):
                first = min(max(j - 1, 0), n_lat_q - 3)
                pat = 0 if j == 0 else (2 if j == n_lat_q - 1 else 1)
            elif j is not None:
                first = jnp.clip(j - 1, 0, n_lat_q - 3)
                pat = jnp.where(j == 0, 0, jnp.where(j == n_lat_q - 1, 2, 1))
            for hl in range(2):
                in_head = (lane >= hl * HEAD_LANES) & (lane < (hl + 1) * HEAD_LANES)
                chunks = [(lambda: kc_ref[...], lambda hl=hl: vc_aug[hl], None, shift)]
                for c in range(NA_BAND // ROW_TILE if j is not None else 0):
                    rows = slice(c * ROW_TILE, (c + 1) * ROW_TILE)

                    def k_chunk(first=first, c=c):
                        start = (first + c) * ROW_TILE
                        if not isinstance(start, int):
                            start = pl.multiple_of(start, ROW_TILE)
                        return kl_ref[pl.ds(start, ROW_TILE), :]

                    chunks.append((k_chunk,
                                   lambda hl=hl, first=first, c=c: vl_aug[hl, first + c],
                                   lambda pat=pat, hl=hl, rows=rows: bias_ref[pat, hl, rows, :],
                                   None))
                qs.append(jnp.where(in_head, q, jnp.zeros_like(q)))
                chunk_lists.append(chunks)
        accs = _flash_transposed(qs, chunk_lists, s_scr, p_scr, bounded)
        return [_normalise_pair(accs[2 * i], accs[2 * i + 1]) for i in range(len(q_tiles))]

    def body(t, carry):
        js = [TILES_PER_STEP * t + i for i in range(TILES_PER_STEP)]
        rows = [pl.ds(pl.multiple_of(j * ROW_TILE, ROW_TILE), ROW_TILE) for j in js]
        outs = tiles([ql_ref[r, :] for r in rows], js)
        for r, o in zip(rows, outs):
            ol_ref[r, :] = o.astype(ol_ref.dtype)
        return carry

    n_loop = n_lat_q // TILES_PER_STEP - 1
    lax.fori_loop(0, n_loop, body, 0)
    js = [n_loop * TILES_PER_STEP + i for i in range(TILES_PER_STEP)]
    rows = [slice(j * ROW_TILE, (j + 1) * ROW_TILE) for j in js]
    outs = tiles([ql_ref[r, :] for r in rows] + [qc_ref[...]], js + [None])
    for r, o in zip(rows, outs):
        ol_ref[r, :] = o.astype(ol_ref.dtype)
    oc_ref[...] = outs[-1].astype(oc_ref.dtype)


def _na_call(nq, nk, nv, bias, shifts, l, n_batch, tiles_per_batch, bounded):
    seq = tiles_per_batch * ROW_TILE
    width = NA_HEADS * NA_DIM
    in_specs, out_specs, scratch = _attention_specs(
        n_batch, tiles_per_batch, LANES, 1, NA_BAND + ROW_TILE)
    in_specs.insert(0, pl.BlockSpec(memory_space=pltpu.SMEM))
    in_specs.append(pl.BlockSpec((None, 3, 2, NA_BAND, ROW_TILE), lambda hp, b: (l, 0, hp, 0, 0)))
    return pl.pallas_call(
        functools.partial(_na_kernel, layer=l, bounded=bounded),
        grid=(NA_HEADS // 2, n_batch),
        in_specs=in_specs,
        out_specs=out_specs,
        out_shape=[jax.ShapeDtypeStruct((n_batch * seq, width), BF),
                   jax.ShapeDtypeStruct((n_batch * ROW_TILE, width), BF)],
        scratch_shapes=scratch,
        compiler_params=pltpu.CompilerParams(
            dimension_semantics=("arbitrary", "arbitrary"), vmem_limit_bytes=VMEM_LIMIT),
        name="na_attention",
    )(shifts, nq, nq, nk, nk, nv, nv, bias)


def _na_window_plan(rows):
    rows_per_tile = ROW_TILE // GRID_W
    n_tiles = rows // rows_per_tile
    band_rows = NA_BAND // GRID_W
    kh = min(NA_KH, rows)
    assert band_rows >= kh + rows_per_tile - 1 and rows_per_tile % 2 == 0
    plan = []
    for j in (0, 1, n_tiles - 1):
        band_start = int(np.clip(j - 1, 0, n_tiles - 3)) * rows_per_tile
        tile_plan = []
        for k_r in range(band_rows):
            key_row = band_start + k_r
            for qp in range(rows_per_tile // 2):
                ok = []
                for r in (j * rows_per_tile + 2 * qp, j * rows_per_tile + 2 * qp + 1):
                    r_start = int(np.clip(r - kh // 2, 0, rows - kh))
                    ok.append(r_start <= key_row < r_start + kh)
                d_left = key_row - (j * rows_per_tile + 2 * qp) + (NA_KH - 1)
                tile_plan.append((k_r, qp, int(np.clip(d_left, 0, 2 * NA_KH - 1)), ok[0], ok[1]))
        plan.append(tile_plan)
    return plan


def _na_bias_kernel(tab_ref, o_ref, *, plan):
    lane = lax.broadcasted_iota(jnp.int32, (1, 2 * GRID_W), 1)
    neg = jnp.full((GRID_W, 2 * GRID_W), NEG_BIG, F32)
    for pi, tile_plan in enumerate(plan):
        for k_r, qp, d_left, left_ok, right_ok in tile_plan:
            if left_ok and right_ok:
                blk = tab_ref[d_left]
            elif left_ok:
                blk = jnp.where(lane < GRID_W, tab_ref[d_left], NEG_BIG)
            elif right_ok:
                blk = jnp.where(lane < GRID_W, NEG_BIG, tab_ref[d_left])
            else:
                blk = neg
            o_ref[pi, k_r * GRID_W:(k_r + 1) * GRID_W, qp * 2 * GRID_W:(qp + 1) * 2 * GRID_W] = blk


def _na_bias_tables(rpb, rows, shifts):
    depth, heads = rpb.shape[:2]
    n_dr = 2 * NA_KH - 1
    k_c = np.arange(GRID_W)[:, None]
    q_c = np.arange(GRID_W)[None, :]
    c_start = np.clip(q_c - NA_KW // 2, 0, GRID_W - NA_KW)
    col_ok = (k_c >= c_start) & (k_c < c_start + NA_KW)
    dc = k_c - q_c + (NA_KW - 1)
    onehot = (np.arange(2 * NA_KW - 1)[:, None, None] == dc[None]).astype(np.float32)
    picked = jnp.einsum("lhdc,ckq->lhdkq", rpb, jnp.asarray(onehot),
                        precision=lax.Precision.HIGHEST)
    shifted = picked * LOG2E - shifts[:, None, None, None, None]
    col_tab = jnp.where(jnp.asarray(col_ok), shifted, NEG_BIG)
    neg = jnp.full((depth, heads, 1, GRID_W, GRID_W), NEG_BIG, F32)
    pair_tab = jnp.concatenate([jnp.concatenate([col_tab, neg], axis=2),
                                jnp.concatenate([neg, col_tab], axis=2)], axis=-1)
    plan = _na_window_plan(rows)
    return pl.pallas_call(
        functools.partial(_na_bias_kernel, plan=plan),
        grid=(depth, heads),
        in_specs=[pl.BlockSpec((None, None, n_dr + 1, GRID_W, 2 * GRID_W),
                               lambda l, h: (l, h, 0, 0, 0))],
        out_specs=pl.BlockSpec((None, 3, None, NA_BAND, ROW_TILE), lambda l, h: (l, 0, h, 0, 0)),
        out_shape=jax.ShapeDtypeStruct((depth, 3, heads, NA_BAND, ROW_TILE), F32),
        compiler_params=pltpu.CompilerParams(
            dimension_semantics=("arbitrary", "arbitrary"), vmem_limit_bytes=VMEM_LIMIT),
        name="na_bias_tables",
    )(pair_tab)


def _merge_mlp_kernel(xl_ref, xc_ref, yml_ref, ymc_ref, ynl_ref, ync_ref, gate_ref, gt1_ref, sh_ref,
                      sc_ref, gt2_ref, w_mo_ref, w_no_ref, w_out_ref, g_ref, w1_ref, w2_ref, o_ref,
                      *, n_lat_tiles):
    is_lat = pl.program_id(0) < n_lat_tiles
    x_in = jnp.where(is_lat, xl_ref[...], xc_ref[...])
    a = _dot(jnp.where(is_lat, yml_ref[...], ymc_ref[...]), w_mo_ref[...])
    b = _dot(jnp.where(is_lat, ynl_ref[...], ync_ref[...]), w_no_ref[...])
    y = gate_ref[:, :D_MODEL].astype(F32) * a + gate_ref[:, D_MODEL:].astype(F32) * b
    x = x_in + gt1_ref[...] * _dot(y.astype(BF), w_out_ref[...])

    h = (_rms(x, g_ref[...] * (1.0 + sc_ref[...])) + sh_ref[...]).astype(BF)
    d_ff = w1_ref.shape[1]
    chunk = 1024
    acc = None
    for c0 in range(0, d_ff, chunk):
        u = jnp.maximum(_dot(h, w1_ref[:, c0:c0 + chunk]), 0.0)
        part = _dot((u * u).astype(BF), w2_ref[c0:c0 + chunk, :])
        acc = part if acc is None else acc + part
    o_ref[...] = x + gt2_ref[...] * acc


def _merge_mlp_call(xs, y_mla, y_na, gate, mod, sw, l, n_tiles, n_lat_tiles, tiles_per_batch,
                    n_batch):
    tm = ROW_TILE
    mod_row = lambda i: jnp.where(i < n_lat_tiles, i // tiles_per_batch, n_batch)
    mod_spec = lambda col: pl.BlockSpec((None, None, 1, D_MODEL),
                                        lambda i: (l, mod_row(i), 0, col))
    row = lambda w: pl.BlockSpec((tm, w), lambda i: (i, 0))
    lat = lambda w: pl.BlockSpec((tm, w), lambda i: (jnp.minimum(i, n_lat_tiles - 1), 0))
    ctx = lambda w: pl.BlockSpec((tm, w), lambda i: (jnp.maximum(i - n_lat_tiles, 0), 0))
    names = ["w_mla_o", "w_na_o", "w_out", "g_mlp", "w_ff1", "w_ff2"]
    return pl.pallas_call(
        functools.partial(_merge_mlp_kernel, n_lat_tiles=n_lat_tiles),
        grid=(n_tiles,),
        in_specs=[*_x_specs(xs, n_lat_tiles), lat(y_mla[0].shape[1]), ctx(y_mla[1].shape[1]),
                  lat(y_na[0].shape[1]), ctx(y_na[1].shape[1]), row(gate.shape[1]),
                  mod_spec(2), mod_spec(3), mod_spec(4), mod_spec(5),
                  *[_layer(sw[n], l) for n in names]],
        out_specs=row(D_MODEL),
        out_shape=jax.ShapeDtypeStruct((n_tiles * tm, D_MODEL), F32),
        compiler_params=pltpu.CompilerParams(
            dimension_semantics=("arbitrary",), vmem_limit_bytes=VMEM_LIMIT),
        name="merge_mlp",
    )(xs[0], xs[1], y_mla[0], y_mla[1], y_na[0], y_na[1], gate, mod, mod, mod, mod,
      *[sw[n] for n in names])


def _take_cols(w, idx):
    idx = np.asarray(idx)
    pieces = []
    start = 0
    while start < len(idx):
        stop = start + 1
        if idx[start] < 0:
            while stop < len(idx) and idx[stop] < 0:
                stop += 1
            pieces.append(jnp.zeros(w.shape[:-1] + (stop - start,), w.dtype))
        else:
            while stop < len(idx) and idx[stop] == idx[stop - 1] + 1:
                stop += 1
            pieces.append(w[..., int(idx[start]):int(idx[stop - 1]) + 1])
        start = stop
    return jnp.concatenate(pieces, axis=-1)


def _pack_weights(w_in, w_uq, w_ukv, g_mla_q, g_mla_k, g_na_q, g_na_k, seq):
    local = _rope_swap_local()
    kr_local = local[MLA_NOPE:] - MLA_NOPE
    in_idx = np.concatenate([
        np.arange(OFF_KR),
        np.full(MLA_NOPE, -1), OFF_KR + kr_local,
        np.arange(OFF_NA, OFF_NA + 3 * NA_HEADS * NA_DIM)])
    uq_idx = np.concatenate([hd * MLA_QK + local for hd in range(MLA_HEADS)])
    per_head = MLA_NOPE + MLA_V
    ukv_k = np.concatenate([np.concatenate([hd * per_head + np.arange(MLA_NOPE),
                                            np.full(LANES - MLA_NOPE, -1)])
                            for hd in range(MLA_HEADS)])
    ukv_v = np.concatenate([hd * per_head + MLA_NOPE + np.arange(MLA_V) for hd in range(MLA_HEADS)])
    pair = np.concatenate([np.arange(NA_DIM), np.arange(NA_DIM)])

    cos_base, sin_base = _rope_tables(seq)
    partner = (np.arange(LANES) + MLA_ROPE) % LANES

    def rope_tabs(g, scale):
        g_ext = _take_cols(g, local)
        g_partner = _take_cols(g, local[partner])
        return (cos_base[None] * (g_ext * scale)[:, None, :],
                sin_base[None] * (g_partner * scale)[:, None, :])

    qcos, qsin = rope_tabs(g_mla_q, LOG2E / math.sqrt(MLA_QK))
    kcos, ksin = rope_tabs(g_mla_k, 1.0)
    return {
        "w_in": _take_cols(w_in.astype(BF), in_idx),
        "w_uq": _take_cols(w_uq.astype(BF), uq_idx),
        "w_ukv": _take_cols(w_ukv.astype(BF), np.concatenate([ukv_k, ukv_v])),
        "qcos": qcos, "qsin": qsin, "kcos": kcos, "ksin": ksin,
        "g_nq": (_take_cols(g_na_q, pair) * (LOG2E / math.sqrt(NA_DIM)))[:, None, :],
        "g_nk": _take_cols(g_na_k, pair)[:, None, :],
    }


def _rope_tables(seq):
    t = np.arange(seq)
    inv_freq = jnp.asarray(ROPE_BASE, F32) ** (-jnp.arange(ROPE_PAIRS, dtype=F32) / ROPE_PAIRS)
    ang_r = jnp.asarray(t // GRID_W, F32)[:, None] * inv_freq
    ang_c = jnp.asarray(t % GRID_W, F32)[:, None] * inv_freq
    cr, sr, cc, sn = jnp.cos(ang_r), jnp.sin(ang_r), jnp.cos(ang_c), jnp.sin(ang_c)
    ones = jnp.ones((seq, MLA_NOPE), F32)
    zeros = jnp.zeros((seq, MLA_NOPE), F32)
    pad = jnp.zeros((seq, LANES - MLA_QK), F32)
    cos_lat = jnp.concatenate([ones, cr, cr, cc, cc, pad], axis=1)
    sin_lat = jnp.concatenate([zeros, -sr, sr, -sn, sn, pad], axis=1)
    cos_id = jnp.concatenate([jnp.ones((ROW_TILE, MLA_QK), F32),
                              jnp.zeros((ROW_TILE, LANES - MLA_QK), F32)], axis=1)
    sin_id = jnp.zeros((ROW_TILE, LANES), F32)
    return jnp.concatenate([cos_lat, cos_id], axis=0), jnp.concatenate([sin_lat, sin_id], axis=0)


def kernel(x, c, ctx, c_ctx, w_ada, b_ada, g_attn, w_in, g_qa, w_uq, g_kva, w_ukv, g_mla_q,
           g_mla_k, g_na_q, g_na_k, rpb, w_mla_o, w_na_o, w_out, g_mlp, w_ff1, w_ff2):
    n_batch, seq, d = x.shape
    ctx_len = ctx.shape[1]
    depth = w_ada.shape[0]
    assert d == D_MODEL and ctx_len == ROW_TILE and seq % (TILES_PER_STEP * ROW_TILE) == 0
    tiles_per_batch = seq // ROW_TILE
    assert tiles_per_batch >= 3
    n_lat_tiles = n_batch * tiles_per_batch
    n_tiles = n_lat_tiles + n_batch
    rows_lat = n_batch * seq

    pad_rows = (-(n_batch + 1)) % 8
    cc = jnp.concatenate([c, c_ctx[None, :], jnp.zeros((pad_rows, d), F32)], axis=0)
    mod_all = _ada_call(cc, w_ada, b_ada)

    sw = _pack_weights(w_in, w_uq, w_ukv, g_mla_q, g_mla_k, g_na_q, g_na_k, seq)
    sw.update({
        "g_attn": g_attn[:, None, :], "g_qa": g_qa[:, None, :], "g_kva": g_kva[:, None, :],
        "g_mlp": g_mlp[:, None, :],
        "w_mla_o": w_mla_o.astype(BF), "w_na_o": w_na_o.astype(BF), "w_out": w_out.astype(BF),
        "w_ff1": w_ff1.astype(BF), "w_ff2": w_ff2.astype(BF),
    })
    gmax = lambda g: jnp.max(jnp.abs(g), axis=1)
    slack = 1.02
    mla_shift = slack * LOG2E * math.sqrt(MLA_QK) * gmax(g_mla_q) * gmax(g_mla_k)
    na_shift = (slack * LOG2E * math.sqrt(NA_DIM) * gmax(g_na_q) * gmax(g_na_k)
                + LOG2E * jnp.maximum(jnp.max(rpb, axis=(1, 2, 3)), 0.0))
    bounded_ok = jnp.maximum(jnp.max(mla_shift), jnp.max(na_shift)) <= SHIFT_LIMIT
    lane = np.arange(LANES)
    sw["q_pad"] = jnp.where(lane == MLA_QK, -mla_shift[:, None, None], 0.0).astype(F32)
    sw["k_pad"] = jnp.broadcast_to(jnp.asarray(lane == MLA_QK, F32), (depth, 1, LANES))

    bias_all = _na_bias_tables(rpb, seq // GRID_W, na_shift)
    mod = mod_all.reshape(depth, mod_all.shape[1], 1, 6 * d)
    def trunk(x_lat, x_ctx, bounded):
        xs = (x_lat, x_ctx, 0)
        for l in range(depth):
            gate, mq, mk, mv, nq, nk, nv = _proj_call(
                xs, n_tiles, mod, sw, l, n_lat_tiles, tiles_per_batch, n_batch)
            y_mla = _mla_call(mq, mk, mv, n_batch, tiles_per_batch, bounded)
            y_na = _na_call(nq, nk, nv, bias_all, na_shift, l, n_batch, tiles_per_batch, bounded)
            upd_tiles = n_lat_tiles if l == depth - 1 else n_tiles
            x_all = _merge_mlp_call(xs, y_mla, y_na, gate, mod, sw, l, upd_tiles,
                                    n_lat_tiles, tiles_per_batch, n_batch)
            xs = (x_all, x_all, n_lat_tiles)
        return x_all

    out = lax.cond(bounded_ok, functools.partial(trunk, bounded=True),
                   functools.partial(trunk, bounded=False),
                   x.reshape(rows_lat, d), ctx.reshape(n_batch * ctx_len, d))
    return out.reshape(n_batch, seq, d)
```

```python
import functools
import math

import numpy as np
import jax
import jax.numpy as jnp
from jax import lax
from jax.experimental import pallas as pl
from jax.experimental.pallas import tpu as pltpu

D_MODEL = 1024
GRID_W = 64
MLA_HEADS = 8
MLA_NOPE = 64
MLA_ROPE = 32
MLA_V = 64
MLA_QK = MLA_NOPE + MLA_ROPE
Q_LORA = 768
KV_LORA = 256
NA_HEADS = 8
NA_DIM = 64
NA_KH = 8
NA_KW = 16
N_BRANCH = 2
ROPE_BASE = 10000.0
ROPE_PAIRS = MLA_ROPE // 4
EPS = 1e-6
OFF_CQ = N_BRANCH * D_MODEL
OFF_CKV = OFF_CQ + Q_LORA
OFF_KR = OFF_CKV + KV_LORA
OFF_NA = OFF_KR + MLA_ROPE

LANES = 128
HEAD_LANES = LANES // 2
assert MLA_V == HEAD_LANES and NA_DIM == HEAD_LANES
ROW_TILE = 256
NA_BAND = 3 * ROW_TILE
TILES_PER_STEP = 4
NA_TILES_PER_STEP = 4
SCORE_LOOKAHEAD = 4
NEG_BIG = -1e30
LOG2E = math.log2(math.e)
SHIFT_LIMIT = 40.0
VMEM_LIMIT = 56 * 1024 * 1024

P_GATE = 0
P_CQ = OFF_CQ
P_CKV = P_CQ + Q_LORA
P_KR = P_CKV + KV_LORA
P_NA = P_KR + LANES
P_COLS = P_NA + 3 * NA_HEADS * NA_DIM

BF = jnp.bfloat16
F32 = jnp.float32


def _dot(a, b):
    return jnp.dot(a, b, preferred_element_type=F32)


def _dot_nt(a, b):
    return lax.dot_general(a, b, (((1,), (1,)), ((), ())), preferred_element_type=F32)


def _dot_tn(a, b):
    return lax.dot_general(a, b, (((0,), (0,)), ((), ())), preferred_element_type=F32)


def _rope_swap_local():
    q = MLA_ROPE // 4
    rope = np.arange(MLA_ROPE)
    swapped = np.concatenate([rope[q:2 * q], rope[:q], rope[3 * q:], rope[2 * q:3 * q]])
    return np.concatenate([np.arange(MLA_NOPE), MLA_NOPE + rope, MLA_NOPE + swapped])


def _single(block_shape, index_map):
    return pl.BlockSpec(block_shape, index_map, pipeline_mode=pl.Buffered(1))


def _layer(a, l):
    zeros = (0,) * (a.ndim - 1)
    return _single((None,) + a.shape[1:], lambda *_: (l,) + zeros)


def _ada_kernel(c_ref, w_ref, b_ref, o_ref):
    c = c_ref[...]
    s = (c * jax.nn.sigmoid(c)).astype(BF)
    o_ref[...] = _dot(s, w_ref[...].astype(BF)) + b_ref[...]


def _ada_call(cc, w_ada, b_ada):
    depth = w_ada.shape[0]
    nrow = cc.shape[0]
    ncol = w_ada.shape[2]
    bn = 1024
    return pl.pallas_call(
        _ada_kernel,
        grid=(depth, ncol // bn),
        in_specs=[
            pl.BlockSpec((nrow, D_MODEL), lambda l, j: (0, 0)),
            pl.BlockSpec((None, D_MODEL, bn), lambda l, j: (l, 0, j)),
            pl.BlockSpec((None, 1, bn), lambda l, j: (l, 0, j)),
        ],
        out_specs=pl.BlockSpec((None, nrow, bn), lambda l, j: (l, 0, j)),
        out_shape=jax.ShapeDtypeStruct((depth, nrow, ncol), F32),
        compiler_params=pltpu.CompilerParams(
            dimension_semantics=("arbitrary", "arbitrary"), vmem_limit_bytes=VMEM_LIMIT),
        name="ada_mod",
    )(cc, w_ada, b_ada.reshape(depth, 1, ncol))


def _rms(x, g):
    ms = jnp.mean(x * x, axis=-1, keepdims=True)
    return x * lax.rsqrt(ms + EPS) * g


def _proj_kernel(xl_ref, xc_ref, sh_ref, sc_ref, g_attn_ref, w_in_ref, w_tail_ref, g_qa_ref, w_uq_ref,
                 g_kva_ref, w_ukv_ref, g_nq_ref, g_nk_ref, qpad_ref, kpad_ref, qcos_ref, qsin_ref,
                 kcos_ref, ksin_ref,
                 gate_ref, mq_ref, mk_ref, mv_ref, nq_ref, nk_ref, nv_ref, *, n_lat_tiles):
    x = jnp.where(pl.program_id(0) < n_lat_tiles, xl_ref[...], xc_ref[...])
    h = _rms(x, g_attn_ref[...] * (1.0 + sc_ref[...])) + sh_ref[...]
    hb = h.astype(BF)

    lane = lax.broadcasted_iota(jnp.int32, (1, LANES), 1)
    qk_mean = jnp.where(lane < MLA_QK, 1.0 / MLA_QK, 0.0).astype(F32)

    def head_norm_rope(blk, cos, sin, pad):
        ms = jnp.sum(blk * blk * qk_mean, axis=-1, keepdims=True)
        y = blk * cos + pltpu.roll(blk, LANES - MLA_ROPE, 1) * sin
        return (y * lax.rsqrt(ms + EPS) + pad).astype(BF)

    gate_chunk = (P_CQ - P_GATE) // 4

    def gate_logits(n):
        return _dot(hb, w_in_ref[:, P_GATE + n * gate_chunk:P_GATE + (n + 1) * gate_chunk])

    def store_gate(n, logits):
        gate_ref[:, n * gate_chunk:(n + 1) * gate_chunk] = jax.nn.sigmoid(logits).astype(BF)

    cq = _dot(hb, w_in_ref[:, P_CQ:P_CKV])
    ckv = _dot(hb, w_in_ref[:, P_CKV:P_KR])
    kr = _dot(hb, w_tail_ref[:, :P_NA - P_KR])
    na = _dot(hb, w_tail_ref[:, P_NA - P_KR:])
    mq = _dot(_rms(cq, g_qa_ref[...]).astype(BF), w_uq_ref[...])
    ckvn = _rms(ckv, g_kva_ref[...]).astype(BF)
    kk = _dot(ckvn, w_ukv_ref[:, :MLA_HEADS * LANES])
    vv = _dot(ckvn, w_ukv_ref[:, MLA_HEADS * LANES:])
    gate0 = gate_logits(0)

    width = NA_HEADS * NA_DIM
    is_lo = lane < NA_DIM
    lo_mean = jnp.where(is_lo, 1.0 / NA_DIM, 0.0).astype(F32)
    hi_mean = jnp.where(is_lo, 0.0, 1.0 / NA_DIM).astype(F32)

    def pair_norm(blk, g):
        sq = blk * blk
        ms_lo = jnp.sum(sq * lo_mean, axis=-1, keepdims=True)
        ms_hi = jnp.sum(sq * hi_mean, axis=-1, keepdims=True)
        r = lax.rsqrt(jnp.where(is_lo, ms_lo, ms_hi) + EPS)
        return (blk * g * r).astype(BF)

    g_nq = g_nq_ref[...]
    g_nk = g_nk_ref[...]
    for pr in range(width // LANES):
        sl = slice(pr * LANES, (pr + 1) * LANES)
        nq_ref[:, sl] = pair_norm(na[:, sl], g_nq)
        nk_ref[:, sl] = pair_norm(na[:, width + pr * LANES: width + (pr + 1) * LANES], g_nk)
    nv_ref[...] = na[:, 2 * width:].astype(BF)
    gate1 = gate_logits(1)

    qcos = qcos_ref[...]
    qsin = qsin_ref[...]
    for hd in range(MLA_HEADS):
        sl = slice(hd * LANES, (hd + 1) * LANES)
        mq_ref[:, sl] = head_norm_rope(mq[:, sl], qcos, qsin, qpad_ref[...])
    store_gate(0, gate0)
    gate2 = gate_logits(2)

    mv_ref[...] = vv.astype(BF)
    kcos = kcos_ref[...]
    ksin = ksin_ref[...]
    for hd in range(MLA_HEADS):
        sl = slice(hd * LANES, (hd + 1) * LANES)
        mk_ref[:, sl] = head_norm_rope(kk[:, sl] + kr, kcos, ksin, kpad_ref[...])
    store_gate(1, gate1)
    gate3 = gate_logits(3)
    store_gate(2, gate2)
    store_gate(3, gate3)


def _x_specs(xs, n_lat_tiles):
    ctx_base = xs[2]
    return [pl.BlockSpec((ROW_TILE, D_MODEL), lambda i: (jnp.minimum(i, n_lat_tiles - 1), 0)),
            pl.BlockSpec((ROW_TILE, D_MODEL),
                         lambda i: (ctx_base + jnp.maximum(i - n_lat_tiles, 0), 0))]


def _proj_call(xs, n_tiles, mod, sw, l, n_lat_tiles, tiles_per_batch, n_batch):
    rows = n_tiles * ROW_TILE
    tm = ROW_TILE

    def mod_row(i):
        return jnp.where(i < n_lat_tiles, i // tiles_per_batch, n_batch)

    def rope_blk(i):
        return jnp.where(i < n_lat_tiles, i % tiles_per_batch, tiles_per_batch)

    names = ["g_attn", "w_in_head", "w_in_tail", "g_qa", "w_uq", "g_kva", "w_ukv", "g_nq", "g_nk",
             "q_pad", "k_pad"]
    rope = pl.BlockSpec((None, tm, LANES), lambda i: (l, rope_blk(i), 0))
    in_specs = [
        *_x_specs(xs, n_lat_tiles),
        pl.BlockSpec((None, None, 1, D_MODEL), lambda i: (l, mod_row(i), 0, 0)),
        pl.BlockSpec((None, None, 1, D_MODEL), lambda i: (l, mod_row(i), 0, 1)),
        *[_layer(sw[n], l) for n in names],
        rope, rope, rope, rope,
    ]
    widths = [N_BRANCH * D_MODEL, MLA_HEADS * LANES, MLA_HEADS * LANES, MLA_HEADS * MLA_V,
              NA_HEADS * NA_DIM, NA_HEADS * NA_DIM, NA_HEADS * NA_DIM]
    out_specs = [pl.BlockSpec((tm, w), lambda i: (i, 0)) for w in widths]
    out_shape = [jax.ShapeDtypeStruct((rows, w), BF) for w in widths]
    return pl.pallas_call(
        functools.partial(_proj_kernel, n_lat_tiles=n_lat_tiles),
        grid=(n_tiles,),
        in_specs=in_specs,
        out_specs=out_specs,
        out_shape=out_shape,
        compiler_params=pltpu.CompilerParams(
            dimension_semantics=("arbitrary",), vmem_limit_bytes=VMEM_LIMIT),
        name="mixer_proj",
    )(xs[0], xs[1], mod, mod, *[sw[n] for n in names], sw["qcos"], sw["qsin"], sw["kcos"],
      sw["ksin"])


def _flash_transposed(qs, chunk_lists, s_scr, p_scr, bounded):
    n_streams = len(qs)
    n_chunks = [len(chunks) for chunks in chunk_lists]
    accs = [None] * n_streams

    def chunk_scores(i, c):
        k, _, bias, shift = chunk_lists[i][c]
        s = _dot_nt(k(), qs[i])
        if bias is not None:
            s = s + bias()
        if shift is not None:
            s = s - shift
        return s

    def chunk_values(i, c):
        return chunk_lists[i][c][1]()

    if bounded:
        order = [(i, c) for c in range(max(n_chunks)) for i in range(n_streams)
                 if c < n_chunks[i]]
        pending = []
        for n in range(len(order) + SCORE_LOOKAHEAD):
            if n < len(order):
                pending.append(chunk_scores(*order[n]))
            if n >= SCORE_LOOKAHEAD:
                i, c = order[n - SCORE_LOOKAHEAD]
                part = _dot(chunk_values(i, c), jnp.exp2(pending.pop(0)).astype(BF))
                accs[i] = part if accs[i] is None else accs[i] + part
        return accs

    maxes = [None] * n_streams

    def stage_scores(i):
        piece_max = []
        for c in range(n_chunks[i]):
            s = chunk_scores(i, c)
            s_scr[i, c * ROW_TILE:(c + 1) * ROW_TILE, :] = s
            piece_max.append(jnp.max(s, axis=0, keepdims=True))
        maxes[i] = functools.reduce(jnp.maximum, piece_max)

    def stage_softmax(i):
        n_keys = n_chunks[i] * ROW_TILE
        p_scr[i, :n_keys, :] = jnp.exp2((s_scr[i, :n_keys, :] - maxes[i]).astype(BF))

    def stage_values(i):
        parts = [_dot(chunk_values(i, c), p_scr[i, c * ROW_TILE:(c + 1) * ROW_TILE, :])
                 for c in range(n_chunks[i])]
        accs[i] = functools.reduce(lambda a, b: a + b, parts)

    for t in range(n_streams + 2):
        if t < n_streams:
            stage_scores(t)
        if 0 <= t - 1 < n_streams:
            stage_softmax(t - 1)
        if 0 <= t - 2 < n_streams:
            stage_values(t - 2)
    return accs


def _ones_outside_head(v, hl):
    lane = lax.broadcasted_iota(jnp.int32, (1, LANES), 1)
    own = (lane >= hl * HEAD_LANES) & (lane < (hl + 1) * HEAD_LANES)
    return jnp.where(own, v, jnp.ones_like(v))


def _normalise_pair(acc0, acc1):
    num = jnp.concatenate([acc0[:HEAD_LANES], acc1[HEAD_LANES:]], axis=0)
    den = jnp.concatenate([acc0[HEAD_LANES:], acc1[:HEAD_LANES]], axis=0)
    return (num / den).T


def _augment_values(vl_ref, vc_ref, vl_aug, vc_aug):
    for hl in range(2):
        for t in range(vl_ref.shape[0] // ROW_TILE):
            v = vl_ref[t * ROW_TILE:(t + 1) * ROW_TILE, :]
            vl_aug[hl, t] = _ones_outside_head(v, hl).astype(F32).T.astype(BF)
        vc_aug[hl] = _ones_outside_head(vc_ref[...], hl).astype(F32).T.astype(BF)


def _mla_kernel(ql_ref, qc_ref, kl_ref, kc_ref, vl_ref, vc_ref, ol_ref, oc_ref, vl_aug, vc_aug,
                s_scr, p_scr, *, bounded):
    seq = kl_ref.shape[0]
    _augment_values(vl_ref, vc_ref, vl_aug, vc_aug)

    def tiles(q_tiles, latent_flags):
        qs, chunk_lists = [], []
        for q, with_latent in zip(q_tiles, latent_flags):
            for hl in range(2):
                sl = slice(hl * LANES, (hl + 1) * LANES)
                chunks = [(lambda sl=sl: kc_ref[:, sl], lambda hl=hl: vc_aug[hl], None, None)]
                for t in range(seq // ROW_TILE if with_latent else 0):
                    rows = slice(t * ROW_TILE, (t + 1) * ROW_TILE)
                    chunks.append((lambda rows=rows, sl=sl: kl_ref[rows, sl],
                                   lambda hl=hl, t=t: vl_aug[hl, t], None, None))
                qs.append(q[:, sl])
                chunk_lists.append(chunks)
        accs = _flash_transposed(qs, chunk_lists, s_scr, p_scr, bounded)
        return [_normalise_pair(accs[2 * i], accs[2 * i + 1]) for i in range(len(q_tiles))]

    def body(t, carry):
        rows = [pl.ds(pl.multiple_of((TILES_PER_STEP * t + i) * ROW_TILE, ROW_TILE), ROW_TILE)
                for i in range(TILES_PER_STEP)]
        outs = tiles([ql_ref[r, :] for r in rows], [True] * TILES_PER_STEP)
        for r, o in zip(rows, outs):
            ol_ref[r, :] = o.astype(ol_ref.dtype)
        return carry

    n_loop = seq // (TILES_PER_STEP * ROW_TILE) - 1
    lax.fori_loop(0, n_loop, body, 0)
    rows = [slice((n_loop * TILES_PER_STEP + i) * ROW_TILE,
                  (n_loop * TILES_PER_STEP + i + 1) * ROW_TILE) for i in range(TILES_PER_STEP)]
    outs = tiles([ql_ref[r, :] for r in rows] + [qc_ref[...]], [True] * TILES_PER_STEP + [False])
    for r, o in zip(rows, outs):
        ol_ref[r, :] = o.astype(ol_ref.dtype)
    oc_ref[...] = outs[-1].astype(oc_ref.dtype)


def _attention_specs(n_batch, tiles_per_batch, q_lanes, batch_axis, max_keys, tiles_per_step):
    tq = ROW_TILE
    seq = tiles_per_batch * tq
    n_lat_tiles = n_batch * tiles_per_batch

    def lat(width):
        return pl.BlockSpec((seq, width), lambda *g: (g[batch_axis], g[1 - batch_axis]))

    def ctx(width):
        return pl.BlockSpec((tq, width),
                            lambda *g: (n_lat_tiles + g[batch_axis], g[1 - batch_axis]))

    in_specs = [lat(q_lanes), ctx(q_lanes), lat(q_lanes), ctx(q_lanes), lat(LANES), ctx(LANES)]
    out_specs = [lat(LANES),
                 pl.BlockSpec((tq, LANES), lambda *g: (g[batch_axis], g[1 - batch_axis]))]
    n_streams = 2 * (tiles_per_step + 1)
    scratch = [pltpu.VMEM((2, tiles_per_batch, LANES, tq), BF), pltpu.VMEM((2, LANES, tq), BF),
               pltpu.VMEM((n_streams, max_keys, tq), F32), pltpu.VMEM((n_streams, max_keys, tq), BF)]
    return in_specs, out_specs, scratch


def _mla_call(mq, mk, mv, n_batch, tiles_per_batch, bounded):
    seq = tiles_per_batch * ROW_TILE
    width = MLA_HEADS * MLA_V
    in_specs, out_specs, scratch = _attention_specs(
        n_batch, tiles_per_batch, 2 * LANES, 0, seq + ROW_TILE, TILES_PER_STEP)
    return pl.pallas_call(
        functools.partial(_mla_kernel, bounded=bounded),
        grid=(n_batch, MLA_HEADS // 2),
        in_specs=in_specs,
        out_specs=out_specs,
        out_shape=[jax.ShapeDtypeStruct((n_batch * seq, width), BF),
                   jax.ShapeDtypeStruct((n_batch * ROW_TILE, width), BF)],
        scratch_shapes=scratch,
        compiler_params=pltpu.CompilerParams(
            dimension_semantics=("arbitrary", "arbitrary"), vmem_limit_bytes=VMEM_LIMIT),
        name="mla_attention",
    )(mq, mq, mk, mk, mv, mv)


def _na_kernel(shift_ref, ql_ref, qc_ref, kl_ref, kc_ref, vl_ref, vc_ref, bias_ref, ol_ref, oc_ref,
               vl_aug, vc_aug, s_scr, p_scr, *, layer, bounded):
    n_lat_q = kl_ref.shape[0] // ROW_TILE
    lane = lax.broadcasted_iota(jnp.int32, (1, LANES), 1)
    shift = shift_ref[layer]
    _augment_values(vl_ref, vc_ref, vl_aug, vc_aug)

    def tiles(q_tiles, js):
        qs, chunk_lists = [], []
        for q, j in zip(q_tiles, js):
            if isinstance(j, int):
                first = min(max(j - 1, 0), n_lat_q - 3)
                pat = 0 if j == 0 else (2 if j == n_lat_q - 1 else 1)
            elif j is not None:
                first = jnp.clip(j - 1, 0, n_lat_q - 3)
                pat = jnp.where(j == 0, 0, jnp.where(j == n_lat_q - 1, 2, 1))
            for hl in range(2):
                in_head = (lane >= hl * HEAD_LANES) & (lane < (hl + 1) * HEAD_LANES)
                chunks = [(lambda: kc_ref[...], lambda hl=hl: vc_aug[hl], None, shift)]
                for c in range(NA_BAND // ROW_TILE if j is not None else 0):
                    rows = slice(c * ROW_TILE, (c + 1) * ROW_TILE)

                    def k_chunk(first=first, c=c):
                        start = (first + c) * ROW_TILE
                        if not isinstance(start, int):
                            start = pl.multiple_of(start, ROW_TILE)
                        return kl_ref[pl.ds(start, ROW_TILE), :]

                    chunks.append((k_chunk,
                                   lambda hl=hl, first=first, c=c: vl_aug[hl, first + c],
                                   lambda pat=pat, hl=hl, rows=rows: bias_ref[pat, hl, rows, :],
                                   None))
                qs.append(jnp.where(in_head, q, jnp.zeros_like(q)))
                chunk_lists.append(chunks)
        accs = _flash_transposed(qs, chunk_lists, s_scr, p_scr, bounded)
        return [_normalise_pair(accs[2 * i], accs[2 * i + 1]) for i in range(len(q_tiles))]

    def body(t, carry):
        js = [NA_TILES_PER_STEP * t + i for i in range(NA_TILES_PER_STEP)]
        rows = [pl.ds(pl.multiple_of(j * ROW_TILE, ROW_TILE), ROW_TILE) for j in js]
        outs = tiles([ql_ref[r, :] for r in rows], js)
        for r, o in zip(rows, outs):
            ol_ref[r, :] = o.astype(ol_ref.dtype)
        return carry

    n_loop = n_lat_q // NA_TILES_PER_STEP - 1
    lax.fori_loop(0, n_loop, body, 0)
    js = [n_loop * NA_TILES_PER_STEP + i for i in range(NA_TILES_PER_STEP)]
    rows = [slice(j * ROW_TILE, (j + 1) * ROW_TILE) for j in js]
    outs = tiles([ql_ref[r, :] for r in rows] + [qc_ref[...]], js + [None])
    for r, o in zip(rows, outs):
        ol_ref[r, :] = o.astype(ol_ref.dtype)
    oc_ref[...] = outs[-1].astype(oc_ref.dtype)


def _na_call(nq, nk, nv, bias, shifts, l, n_batch, tiles_per_batch, bounded):
    seq = tiles_per_batch * ROW_TILE
    width = NA_HEADS * NA_DIM
    in_specs, out_specs, scratch = _attention_specs(
        n_batch, tiles_per_batch, LANES, 1, NA_BAND + ROW_TILE, NA_TILES_PER_STEP)
    in_specs.insert(0, pl.BlockSpec(memory_space=pltpu.SMEM))
    in_specs.append(pl.BlockSpec((None, 3, 2, NA_BAND, ROW_TILE), lambda hp, b: (l, 0, hp, 0, 0)))
    return pl.pallas_call(
        functools.partial(_na_kernel, layer=l, bounded=bounded),
        grid=(NA_HEADS // 2, n_batch),
        in_specs=in_specs,
        out_specs=out_specs,
        out_shape=[jax.ShapeDtypeStruct((n_batch * seq, width), BF),
                   jax.ShapeDtypeStruct((n_batch * ROW_TILE, width), BF)],
        scratch_shapes=scratch,
        compiler_params=pltpu.CompilerParams(
            dimension_semantics=("arbitrary", "arbitrary"), vmem_limit_bytes=VMEM_LIMIT),
        name="na_attention",
    )(shifts, nq, nq, nk, nk, nv, nv, bias)


def _na_window_plan(rows):
    rows_per_tile = ROW_TILE // GRID_W
    n_tiles = rows // rows_per_tile
    band_rows = NA_BAND // GRID_W
    kh = min(NA_KH, rows)
    assert band_rows >= kh + rows_per_tile - 1 and rows_per_tile % 2 == 0
    plan = []
    for j in (0, 1, n_tiles - 1):
        band_start = int(np.clip(j - 1, 0, n_tiles - 3)) * rows_per_tile
        tile_plan = []
        for k_r in range(band_rows):
            key_row = band_start + k_r
            for qp in range(rows_per_tile // 2):
                ok = []
                for r in (j * rows_per_tile + 2 * qp, j * rows_per_tile + 2 * qp + 1):
                    r_start = int(np.clip(r - kh // 2, 0, rows - kh))
                    ok.append(r_start <= key_row < r_start + kh)
                d_left = key_row - (j * rows_per_tile + 2 * qp) + (NA_KH - 1)
                tile_plan.append((k_r, qp, int(np.clip(d_left, 0, 2 * NA_KH - 1)), ok[0], ok[1]))
        plan.append(tile_plan)
    return plan


def _na_bias_kernel(tab_ref, o_ref, *, plan):
    lane = lax.broadcasted_iota(jnp.int32, (1, 2 * GRID_W), 1)
    neg = jnp.full((GRID_W, 2 * GRID_W), NEG_BIG, F32)
    for pi, tile_plan in enumerate(plan):
        for k_r, qp, d_left, left_ok, right_ok in tile_plan:
            if left_ok and right_ok:
                blk = tab_ref[d_left]
            elif left_ok:
                blk = jnp.where(lane < GRID_W, tab_ref[d_left], NEG_BIG)
            elif right_ok:
                blk = jnp.where(lane < GRID_W, NEG_BIG, tab_ref[d_left])
            else:
                blk = neg
            o_ref[pi, k_r * GRID_W:(k_r + 1) * GRID_W, qp * 2 * GRID_W:(qp + 1) * 2 * GRID_W] = blk


def _na_bias_tables(rpb, rows, shifts):
    depth, heads = rpb.shape[:2]
    n_dr = 2 * NA_KH - 1
    k_c = np.arange(GRID_W)[:, None]
    q_c = np.arange(GRID_W)[None, :]
    c_start = np.clip(q_c - NA_KW // 2, 0, GRID_W - NA_KW)
    col_ok = (k_c >= c_start) & (k_c < c_start + NA_KW)
    dc = k_c - q_c + (NA_KW - 1)
    onehot = (np.arange(2 * NA_KW - 1)[:, None, None] == dc[None]).astype(np.float32)
    picked = jnp.einsum("lhdc,ckq->lhdkq", rpb, jnp.asarray(onehot),
                        precision=lax.Precision.HIGHEST)
    shifted = picked * LOG2E - shifts[:, None, None, None, None]
    col_tab = jnp.where(jnp.asarray(col_ok), shifted, NEG_BIG)
    neg = jnp.full((depth, heads, 1, GRID_W, GRID_W), NEG_BIG, F32)
    pair_tab = jnp.concatenate([jnp.concatenate([col_tab, neg], axis=2),
                                jnp.concatenate([neg, col_tab], axis=2)], axis=-1)
    plan = _na_window_plan(rows)
    return pl.pallas_call(
        functools.partial(_na_bias_kernel, plan=plan),
        grid=(depth, heads),
        in_specs=[pl.BlockSpec((None, None, n_dr + 1, GRID_W, 2 * GRID_W),
                               lambda l, h: (l, h, 0, 0, 0))],
        out_specs=pl.BlockSpec((None, 3, None, NA_BAND, ROW_TILE), lambda l, h: (l, 0, h, 0, 0)),
        out_shape=jax.ShapeDtypeStruct((depth, 3, heads, NA_BAND, ROW_TILE), F32),
        compiler_params=pltpu.CompilerParams(
            dimension_semantics=("arbitrary", "arbitrary"), vmem_limit_bytes=VMEM_LIMIT),
        name="na_bias_tables",
    )(pair_tab)


def _merge_mlp_kernel(xl_ref, xc_ref, yml_ref, ymc_ref, ynl_ref, ync_ref, gate_ref, gt1_ref, sh_ref,
                      sc_ref, gt2_ref, w_mo_ref, w_no_ref, w_out_ref, g_ref, w1_ref, w2_ref, o_ref,
                      *, n_lat_tiles):
    is_lat = pl.program_id(0) < n_lat_tiles
    x_in = jnp.where(is_lat, xl_ref[...], xc_ref[...])
    a = _dot(jnp.where(is_lat, yml_ref[...], ymc_ref[...]), w_mo_ref[...])
    b = _dot(jnp.where(is_lat, ynl_ref[...], ync_ref[...]), w_no_ref[...])
    y = gate_ref[:, :D_MODEL].astype(F32) * a + gate_ref[:, D_MODEL:].astype(F32) * b
    x = x_in + gt1_ref[...] * _dot(y.astype(BF), w_out_ref[...])

    h = (_rms(x, g_ref[...] * (1.0 + sc_ref[...])) + sh_ref[...]).astype(BF)
    d_ff = w1_ref.shape[1]
    chunk = 1024
    acc = None
    for c0 in range(0, d_ff, chunk):
        u = jnp.maximum(_dot(h, w1_ref[:, c0:c0 + chunk]), 0.0)
        part = _dot((u * u).astype(BF), w2_ref[c0:c0 + chunk, :])
        acc = part if acc is None else acc + part
    o_ref[...] = x + gt2_ref[...] * acc


def _merge_mlp_call(xs, y_mla, y_na, gate, mod, sw, l, n_tiles, n_lat_tiles, tiles_per_batch,
                    n_batch):
    tm = ROW_TILE
    mod_row = lambda i: jnp.where(i < n_lat_tiles, i // tiles_per_batch, n_batch)
    mod_spec = lambda col: pl.BlockSpec((None, None, 1, D_MODEL),
                                        lambda i: (l, mod_row(i), 0, col))
    row = lambda w: pl.BlockSpec((tm, w), lambda i: (i, 0))
    lat = lambda w: pl.BlockSpec((tm, w), lambda i: (jnp.minimum(i, n_lat_tiles - 1), 0))
    ctx = lambda w: pl.BlockSpec((tm, w), lambda i: (jnp.maximum(i - n_lat_tiles, 0), 0))
    names = ["w_mla_o", "w_na_o", "w_out", "g_mlp", "w_ff1", "w_ff2"]
    return pl.pallas_call(
        functools.partial(_merge_mlp_kernel, n_lat_tiles=n_lat_tiles),
        grid=(n_tiles,),
        in_specs=[*_x_specs(xs, n_lat_tiles), lat(y_mla[0].shape[1]), ctx(y_mla[1].shape[1]),
                  lat(y_na[0].shape[1]), ctx(y_na[1].shape[1]), row(gate.shape[1]),
                  mod_spec(2), mod_spec(3), mod_spec(4), mod_spec(5),
                  *[_layer(sw[n], l) for n in names]],
        out_specs=row(D_MODEL),
        out_shape=jax.ShapeDtypeStruct((n_tiles * tm, D_MODEL), F32),
        compiler_params=pltpu.CompilerParams(
            dimension_semantics=("arbitrary",), vmem_limit_bytes=VMEM_LIMIT),
        name="merge_mlp",
    )(xs[0], xs[1], y_mla[0], y_mla[1], y_na[0], y_na[1], gate, mod, mod, mod, mod,
      *[sw[n] for n in names])


def _take_cols(w, idx):
    idx = np.asarray(idx)
    pieces = []
    start = 0
    while start < len(idx):
        stop = start + 1
        if idx[start] < 0:
            while stop < len(idx) and idx[stop] < 0:
                stop += 1
            pieces.append(jnp.zeros(w.shape[:-1] + (stop - start,), w.dtype))
        else:
            while stop < len(idx) and idx[stop] == idx[stop - 1] + 1:
                stop += 1
            pieces.append(w[..., int(idx[start]):int(idx[stop - 1]) + 1])
        start = stop
    return jnp.concatenate(pieces, axis=-1)


def _pack_weights(w_in, w_uq, w_ukv, g_mla_q, g_mla_k, g_na_q, g_na_k, seq):
    local = _rope_swap_local()
    kr_local = local[MLA_NOPE:] - MLA_NOPE
    tail_idx = np.concatenate([
        np.full(MLA_NOPE, -1), kr_local,
        np.arange(MLA_ROPE, MLA_ROPE + 3 * NA_HEADS * NA_DIM)])
    uq_idx = np.concatenate([hd * MLA_QK + local for hd in range(MLA_HEADS)])
    per_head = MLA_NOPE + MLA_V
    ukv_k = np.concatenate([np.concatenate([hd * per_head + np.arange(MLA_NOPE),
                                            np.full(LANES - MLA_NOPE, -1)])
                            for hd in range(MLA_HEADS)])
    ukv_v = np.concatenate([hd * per_head + MLA_NOPE + np.arange(MLA_V) for hd in range(MLA_HEADS)])
    pair = np.concatenate([np.arange(NA_DIM), np.arange(NA_DIM)])

    cos_base, sin_base = _rope_tables(seq)
    partner = (np.arange(LANES) + MLA_ROPE) % LANES

    def rope_tabs(g, scale):
        g_ext = _take_cols(g, local)
        g_partner = _take_cols(g, local[partner])
        return (cos_base[None] * (g_ext * scale)[:, None, :],
                sin_base[None] * (g_partner * scale)[:, None, :])

    qcos, qsin = rope_tabs(g_mla_q, LOG2E / math.sqrt(MLA_QK))
    kcos, ksin = rope_tabs(g_mla_k, 1.0)
    return {
        "w_in_head": w_in[:, :, :OFF_KR].astype(BF),
        "w_in_tail": _take_cols(w_in[:, :, OFF_KR:].astype(BF), tail_idx),
        "w_uq": _take_cols(w_uq.astype(BF), uq_idx),
        "w_ukv": _take_cols(w_ukv.astype(BF), np.concatenate([ukv_k, ukv_v])),
        "qcos": qcos, "qsin": qsin, "kcos": kcos, "ksin": ksin,
        "g_nq": (_take_cols(g_na_q, pair) * (LOG2E / math.sqrt(NA_DIM)))[:, None, :],
        "g_nk": _take_cols(g_na_k, pair)[:, None, :],
    }


def _rope_tables(seq):
    t = np.arange(seq)
    inv_freq = jnp.asarray(ROPE_BASE, F32) ** (-jnp.arange(ROPE_PAIRS, dtype=F32) / ROPE_PAIRS)
    ang_r = jnp.asarray(t // GRID_W, F32)[:, None] * inv_freq
    ang_c = jnp.asarray(t % GRID_W, F32)[:, None] * inv_freq
    cr, sr, cc, sn = jnp.cos(ang_r), jnp.sin(ang_r), jnp.cos(ang_c), jnp.sin(ang_c)
    ones = jnp.ones((seq, MLA_NOPE), F32)
    zeros = jnp.zeros((seq, MLA_NOPE), F32)
    pad = jnp.zeros((seq, LANES - MLA_QK), F32)
    cos_lat = jnp.concatenate([ones, cr, cr, cc, cc, pad], axis=1)
    sin_lat = jnp.concatenate([zeros, -sr, sr, -sn, sn, pad], axis=1)
    cos_id = jnp.concatenate([jnp.ones((ROW_TILE, MLA_QK), F32),
                              jnp.zeros((ROW_TILE, LANES - MLA_QK), F32)], axis=1)
    sin_id = jnp.zeros((ROW_TILE, LANES), F32)
    return jnp.concatenate([cos_lat, cos_id], axis=0), jnp.concatenate([sin_lat, sin_id], axis=0)


def kernel(x, c, ctx, c_ctx, w_ada, b_ada, g_attn, w_in, g_qa, w_uq, g_kva, w_ukv, g_mla_q,
           g_mla_k, g_na_q, g_na_k, rpb, w_mla_o, w_na_o, w_out, g_mlp, w_ff1, w_ff2):
    n_batch, seq, d = x.shape
    ctx_len = ctx.shape[1]
    depth = w_ada.shape[0]
    assert d == D_MODEL and ctx_len == ROW_TILE
    assert seq % (TILES_PER_STEP * ROW_TILE) == 0 and seq % (NA_TILES_PER_STEP * ROW_TILE) == 0
    tiles_per_batch = seq // ROW_TILE
    assert tiles_per_batch >= 3
    n_lat_tiles = n_batch * tiles_per_batch
    n_tiles = n_lat_tiles + n_batch
    rows_lat = n_batch * seq

    pad_rows = (-(n_batch + 1)) % 8
    cc = jnp.concatenate([c, c_ctx[None, :], jnp.zeros((pad_rows, d), F32)], axis=0)
    mod_all = _ada_call(cc, w_ada, b_ada)

    sw = _pack_weights(w_in, w_uq, w_ukv, g_mla_q, g_mla_k, g_na_q, g_na_k, seq)
    sw.update({
        "g_attn": g_attn[:, None, :], "g_qa": g_qa[:, None, :], "g_kva": g_kva[:, None, :],
        "g_mlp": g_mlp[:, None, :],
        "w_mla_o": w_mla_o.astype(BF), "w_na_o": w_na_o.astype(BF), "w_out": w_out.astype(BF),
        "w_ff1": w_ff1.astype(BF), "w_ff2": w_ff2.astype(BF),
    })
    gmax = lambda g: jnp.max(jnp.abs(g), axis=1)
    slack = 1.02
    mla_shift = slack * LOG2E * math.sqrt(MLA_QK) * gmax(g_mla_q) * gmax(g_mla_k)
    na_shift = (slack * LOG2E * math.sqrt(NA_DIM) * gmax(g_na_q) * gmax(g_na_k)
                + LOG2E * jnp.maximum(jnp.max(rpb, axis=(1, 2, 3)), 0.0))
    bounded_ok = jnp.maximum(jnp.max(mla_shift), jnp.max(na_shift)) <= SHIFT_LIMIT
    lane = np.arange(LANES)
    sw["q_pad"] = jnp.where(lane == MLA_QK, -mla_shift[:, None, None], 0.0).astype(F32)
    sw["k_pad"] = jnp.broadcast_to(jnp.asarray(lane == MLA_QK, F32), (depth, 1, LANES))

    bias_all = _na_bias_tables(rpb, seq // GRID_W, na_shift)
    mod = mod_all.reshape(depth, mod_all.shape[1], 1, 6 * d)
    def trunk(x_lat, x_ctx, bounded):
        xs = (x_lat, x_ctx, 0)
        for l in range(depth):
            gate, mq, mk, mv, nq, nk, nv = _proj_call(
                xs, n_tiles, mod, sw, l, n_lat_tiles, tiles_per_batch, n_batch)
            y_mla = _mla_call(mq, mk, mv, n_batch, tiles_per_batch, bounded)
            y_na = _na_call(nq, nk, nv, bias_all, na_shift, l, n_batch, tiles_per_batch, bounded)
            upd_tiles = n_lat_tiles if l == depth - 1 else n_tiles
            x_all = _merge_mlp_call(xs, y_mla, y_na, gate, mod, sw, l, upd_tiles,
                                    n_lat_tiles, tiles_per_batch, n_batch)
            xs = (x_all, x_all, n_lat_tiles)
        return x_all

    out = lax.cond(bounded_ok, functools.partial(trunk, bounded=True),
                   functools.partial(trunk, bounded=False),
                   x.reshape(rows_lat, d), ctx.reshape(n_batch * ctx_len, d))
    return out.reshape(n_batch, seq, d)
```

```python
import functools
import math

import numpy as np
import jax
import jax.numpy as jnp
from jax import lax
from jax.experimental import pallas as pl
from jax.experimental.pallas import tpu as pltpu

D_MODEL = 1024
GRID_W = 64
MLA_HEADS = 8
MLA_NOPE = 64
MLA_ROPE = 32
MLA_V = 64
MLA_QK = MLA_NOPE + MLA_ROPE
Q_LORA = 768
KV_LORA = 256
NA_HEADS = 8
NA_DIM = 64
NA_KH = 8
NA_KW = 16
N_BRANCH = 2
ROPE_BASE = 10000.0
ROPE_PAIRS = MLA_ROPE // 4
EPS = 1e-6
OFF_CQ = N_BRANCH * D_MODEL
OFF_CKV = OFF_CQ + Q_LORA
OFF_KR = OFF_CKV + KV_LORA
OFF_NA = OFF_KR + MLA_ROPE

LANES = 128
HEAD_LANES = LANES // 2
assert MLA_V == HEAD_LANES and NA_DIM == HEAD_LANES
ROW_TILE = 256
PROJ_TILE = 256
STAGE2_KEY_HEADS = 4
NA_BAND = 3 * ROW_TILE
TILES_PER_STEP = 4
NA_TILES_PER_STEP = 4
SCORE_LOOKAHEAD = 4
NEG_BIG = -1e30
LOG2E = math.log2(math.e)
SHIFT_LIMIT = 40.0
VMEM_LIMIT = 56 * 1024 * 1024

P_GATE = 0
P_CQ = OFF_CQ
P_CKV = P_CQ + Q_LORA
P_KR = P_CKV + KV_LORA
P_NA = P_KR + LANES
P_COLS = P_NA + 3 * NA_HEADS * NA_DIM

BF = jnp.bfloat16
F32 = jnp.float32


def _dot(a, b):
    return jnp.dot(a, b, preferred_element_type=F32)


def _dot_nt(a, b):
    return lax.dot_general(a, b, (((1,), (1,)), ((), ())), preferred_element_type=F32)


def _dot_tn(a, b):
    return lax.dot_general(a, b, (((0,), (0,)), ((), ())), preferred_element_type=F32)


def _rope_swap_local():
    q = MLA_ROPE // 4
    rope = np.arange(MLA_ROPE)
    swapped = np.concatenate([rope[q:2 * q], rope[:q], rope[3 * q:], rope[2 * q:3 * q]])
    return np.concatenate([np.arange(MLA_NOPE), MLA_NOPE + rope, MLA_NOPE + swapped])


def _single(block_shape, index_map):
    return pl.BlockSpec(block_shape, index_map, pipeline_mode=pl.Buffered(1))


def _layer(a, l):
    zeros = (0,) * (a.ndim - 1)
    return _single((None,) + a.shape[1:], lambda *_: (l,) + zeros)


def _ada_kernel(c_ref, w_ref, b_ref, o_ref):
    c = c_ref[...]
    s = (c * jax.nn.sigmoid(c)).astype(BF)
    o_ref[...] = _dot(s, w_ref[...].astype(BF)) + b_ref[...]


def _ada_call(cc, w_ada, b_ada):
    depth = w_ada.shape[0]
    nrow = cc.shape[0]
    ncol = w_ada.shape[2]
    bn = 1024
    return pl.pallas_call(
        _ada_kernel,
        grid=(depth, ncol // bn),
        in_specs=[
            pl.BlockSpec((nrow, D_MODEL), lambda l, j: (0, 0)),
            pl.BlockSpec((None, D_MODEL, bn), lambda l, j: (l, 0, j)),
            pl.BlockSpec((None, 1, bn), lambda l, j: (l, 0, j)),
        ],
        out_specs=pl.BlockSpec((None, nrow, bn), lambda l, j: (l, 0, j)),
        out_shape=jax.ShapeDtypeStruct((depth, nrow, ncol), F32),
        compiler_params=pltpu.CompilerParams(
            dimension_semantics=("arbitrary", "arbitrary"), vmem_limit_bytes=VMEM_LIMIT),
        name="ada_mod",
    )(cc, w_ada, b_ada.reshape(depth, 1, ncol))


def _rms(x, g):
    ms = jnp.mean(x * x, axis=-1, keepdims=True)
    return x * lax.rsqrt(ms + EPS) * g


def _proj_kernel(xl_ref, xc_ref, sh_ref, sc_ref, g_attn_ref, w_in_ref, w_tail_ref, g_qa_ref, w_uq_ref,
                 g_kva_ref, w_ukv_ref, g_nq_ref, g_nk_ref, qpad_ref, kpad_ref, qcos_ref, qsin_ref,
                 kcos_ref, ksin_ref,
                 gate_ref, mq_ref, mk_ref, mv_ref, nq_ref, nk_ref, nv_ref,
                 hb_scr, mq_scr, kk_scr, kr_scr, na_scr, *, n_lat_tiles):
    x = jnp.where(pl.program_id(0) < n_lat_tiles, xl_ref[...], xc_ref[...])
    h = _rms(x, g_attn_ref[...] * (1.0 + sc_ref[...])) + sh_ref[...]
    hb = h.astype(BF)

    lane = lax.broadcasted_iota(jnp.int32, (1, LANES), 1)
    qk_mean = jnp.where(lane < MLA_QK, 1.0 / MLA_QK, 0.0).astype(F32)

    def head_norm_rope(blk, cos, sin, pad):
        ms = jnp.sum(blk * blk * qk_mean, axis=-1, keepdims=True)
        y = blk * cos + pltpu.roll(blk, LANES - MLA_ROPE, 1) * sin
        return (y * lax.rsqrt(ms + EPS) + pad).astype(BF)

    gate_chunk = (P_CQ - P_GATE) // 4

    def gate_logits(n):
        return _dot(hb_scr[...], w_in_ref[:, P_GATE + n * gate_chunk:P_GATE + (n + 1) * gate_chunk])

    def store_gate(n, logits):
        gate_ref[:, n * gate_chunk:(n + 1) * gate_chunk] = jax.nn.sigmoid(logits).astype(BF)

    hb_scr[...] = hb
    cq = _dot(hb, w_in_ref[:, P_CQ:P_CKV])
    ckv = _dot(hb, w_in_ref[:, P_CKV:P_KR])
    kr_scr[...] = _dot(hb, w_tail_ref[:, :P_NA - P_KR])
    mq_scr[...] = _dot(_rms(cq, g_qa_ref[...]).astype(BF), w_uq_ref[...])
    ckvn = _rms(ckv, g_kva_ref[...]).astype(BF)
    kk_scr[...] = _dot(ckvn, w_ukv_ref[:, :MLA_HEADS * LANES])
    mv_ref[...] = _dot(ckvn, w_ukv_ref[:, MLA_HEADS * LANES:]).astype(BF)

    one = jnp.minimum(pl.program_id(0), 0) + 1

    width = NA_HEADS * NA_DIM
    is_lo = lane < NA_DIM
    lo_mean = jnp.where(is_lo, 1.0 / NA_DIM, 0.0).astype(F32)
    hi_mean = jnp.where(is_lo, 0.0, 1.0 / NA_DIM).astype(F32)

    def pair_norm(blk, g):
        sq = blk * blk
        ms_lo = jnp.sum(sq * lo_mean, axis=-1, keepdims=True)
        ms_hi = jnp.sum(sq * hi_mean, axis=-1, keepdims=True)
        r = lax.rsqrt(jnp.where(is_lo, ms_lo, ms_hi) + EPS)
        return (blk * g * r).astype(BF)

    def key_epilogue(heads):
        kcos = kcos_ref[...]
        ksin = ksin_ref[...]
        kr = kr_scr[...]
        for hd in heads:
            sl = slice(hd * LANES, (hd + 1) * LANES)
            mk_ref[:, sl] = head_norm_rope(kk_scr[:, sl] + kr, kcos, ksin, kpad_ref[...])

    def stage2(_, carry):
        na_scr[...] = _dot(hb_scr[...], w_tail_ref[:, P_NA - P_KR:])
        gate0 = gate_logits(0)
        qcos = qcos_ref[...]
        qsin = qsin_ref[...]
        for hd in range(MLA_HEADS):
            sl = slice(hd * LANES, (hd + 1) * LANES)
            mq_ref[:, sl] = head_norm_rope(mq_scr[:, sl], qcos, qsin, qpad_ref[...])
        key_epilogue(range(STAGE2_KEY_HEADS))
        store_gate(0, gate0)
        return carry

    def stage3(_, carry):
        gates = [gate_logits(n) for n in (1, 2, 3)]
        key_epilogue(range(STAGE2_KEY_HEADS, MLA_HEADS))
        g_nq = g_nq_ref[...]
        g_nk = g_nk_ref[...]
        for pr in range(width // LANES):
            sl = slice(pr * LANES, (pr + 1) * LANES)
            sk = slice(width + pr * LANES, width + (pr + 1) * LANES)
            nq_ref[:, sl] = pair_norm(na_scr[:, sl], g_nq)
            nk_ref[:, sl] = pair_norm(na_scr[:, sk], g_nk)
        nv_ref[...] = na_scr[:, 2 * width:].astype(BF)
        for n, logits in zip((1, 2, 3), gates):
            store_gate(n, logits)
        return carry

    lax.fori_loop(0, one, stage2, 0)
    lax.fori_loop(0, one, stage3, 0)


def _x_specs(xs, n_lat_tiles, tile=ROW_TILE):
    ctx_base = xs[2] // tile
    return [pl.BlockSpec((tile, D_MODEL), lambda i: (jnp.minimum(i, n_lat_tiles - 1), 0)),
            pl.BlockSpec((tile, D_MODEL),
                         lambda i: (ctx_base + jnp.maximum(i - n_lat_tiles, 0), 0))]


def _proj_call(xs, n_rows, n_rows_lat, rows_per_batch, mod, sw, l, n_batch):
    rows = n_rows
    tm = PROJ_TILE
    n_tiles = n_rows // tm
    n_lat_tiles = n_rows_lat // tm
    tiles_per_batch = rows_per_batch // tm

    def mod_row(i):
        return jnp.where(i < n_lat_tiles, i // tiles_per_batch, n_batch)

    def rope_blk(i):
        return jnp.where(i < n_lat_tiles, i % tiles_per_batch, tiles_per_batch)

    names = ["g_attn", "w_in_head", "w_in_tail", "g_qa", "w_uq", "g_kva", "w_ukv", "g_nq", "g_nk",
             "q_pad", "k_pad"]
    rope = pl.BlockSpec((None, tm, LANES), lambda i: (l, rope_blk(i), 0))
    in_specs = [
        *_x_specs(xs, n_lat_tiles, tm),
        pl.BlockSpec((None, None, 1, D_MODEL), lambda i: (l, mod_row(i), 0, 0)),
        pl.BlockSpec((None, None, 1, D_MODEL), lambda i: (l, mod_row(i), 0, 1)),
        *[_single((None, D_MODEL, OFF_KR), lambda i: (l, 0, 0)) if n == "w_in_head"
          else _layer(sw[n], l) for n in names],
        rope, rope, rope, rope,
    ]
    widths = [N_BRANCH * D_MODEL, MLA_HEADS * LANES, MLA_HEADS * LANES, MLA_HEADS * MLA_V,
              NA_HEADS * NA_DIM, NA_HEADS * NA_DIM, NA_HEADS * NA_DIM]
    out_specs = [pl.BlockSpec((tm, w), lambda i: (i, 0)) for w in widths]
    out_shape = [jax.ShapeDtypeStruct((rows, w), BF) for w in widths]
    return pl.pallas_call(
        functools.partial(_proj_kernel, n_lat_tiles=n_lat_tiles),
        grid=(n_tiles,),
        in_specs=in_specs,
        out_specs=out_specs,
        out_shape=out_shape,
        scratch_shapes=[pltpu.VMEM((tm, D_MODEL), BF),
                        pltpu.VMEM((tm, MLA_HEADS * LANES), F32),
                        pltpu.VMEM((tm, MLA_HEADS * LANES), F32),
                        pltpu.VMEM((tm, LANES), F32),
                        pltpu.VMEM((tm, 3 * NA_HEADS * NA_DIM), F32)],
        compiler_params=pltpu.CompilerParams(
            dimension_semantics=("arbitrary",), vmem_limit_bytes=VMEM_LIMIT),
        name="mixer_proj",
    )(xs[0], xs[1], mod, mod, *[sw[n] for n in names], sw["qcos"], sw["qsin"], sw["kcos"],
      sw["ksin"])


def _flash_transposed(qs, chunk_lists, s_scr, p_scr, bounded):
    n_streams = len(qs)
    n_chunks = [len(chunks) for chunks in chunk_lists]
    accs = [None] * n_streams

    def chunk_scores(i, c):
        k, _, bias, shift = chunk_lists[i][c]
        s = _dot_nt(k(), qs[i])
        if bias is not None:
            s = s + bias()
        if shift is not None:
            s = s - shift
        return s

    def chunk_values(i, c):
        return chunk_lists[i][c][1]()

    if bounded:
        order = [(i, c) for c in range(max(n_chunks)) for i in range(n_streams)
                 if c < n_chunks[i]]
        pending = []
        for n in range(len(order) + SCORE_LOOKAHEAD):
            if n < len(order):
                pending.append(chunk_scores(*order[n]))
            if n >= SCORE_LOOKAHEAD:
                i, c = order[n - SCORE_LOOKAHEAD]
                part = _dot(chunk_values(i, c), jnp.exp2(pending.pop(0)).astype(BF))
                accs[i] = part if accs[i] is None else accs[i] + part
        return accs

    maxes = [None] * n_streams

    def stage_scores(i):
        piece_max = []
        for c in range(n_chunks[i]):
            s = chunk_scores(i, c)
            s_scr[i, c * ROW_TILE:(c + 1) * ROW_TILE, :] = s
            piece_max.append(jnp.max(s, axis=0, keepdims=True))
        maxes[i] = functools.reduce(jnp.maximum, piece_max)

    def stage_softmax(i):
        n_keys = n_chunks[i] * ROW_TILE
        p_scr[i, :n_keys, :] = jnp.exp2((s_scr[i, :n_keys, :] - maxes[i]).astype(BF))

    def stage_values(i):
        parts = [_dot(chunk_values(i, c), p_scr[i, c * ROW_TILE:(c + 1) * ROW_TILE, :])
                 for c in range(n_chunks[i])]
        accs[i] = functools.reduce(lambda a, b: a + b, parts)

    for t in range(n_streams + 2):
        if t < n_streams:
            stage_scores(t)
        if 0 <= t - 1 < n_streams:
            stage_softmax(t - 1)
        if 0 <= t - 2 < n_streams:
            stage_values(t - 2)
    return accs


def _ones_outside_head(v, hl):
    lane = lax.broadcasted_iota(jnp.int32, (1, LANES), 1)
    own = (lane >= hl * HEAD_LANES) & (lane < (hl + 1) * HEAD_LANES)
    return jnp.where(own, v, jnp.ones_like(v))


def _normalise_pair(acc0, acc1):
    num = jnp.concatenate([acc0[:HEAD_LANES], acc1[HEAD_LANES:]], axis=0)
    den = jnp.concatenate([acc0[HEAD_LANES:], acc1[:HEAD_LANES]], axis=0)
    return (num / den).T


def _augment_values(vl_ref, vc_ref, vl_aug, vc_aug):
    for hl in range(2):
        for t in range(vl_ref.shape[0] // ROW_TILE):
            v = vl_ref[t * ROW_TILE:(t + 1) * ROW_TILE, :]
            vl_aug[hl, t] = _ones_outside_head(v, hl).astype(F32).T.astype(BF)
        vc_aug[hl] = _ones_outside_head(vc_ref[...], hl).astype(F32).T.astype(BF)


def _mla_kernel(ql_ref, qc_ref, kl_ref, kc_ref, vl_ref, vc_ref, ol_ref, oc_ref, vl_aug, vc_aug,
                s_scr, p_scr, *, bounded):
    seq = kl_ref.shape[0]
    _augment_values(vl_ref, vc_ref, vl_aug, vc_aug)

    def tiles(q_tiles, latent_flags):
        qs, chunk_lists = [], []
        for q, with_latent in zip(q_tiles, latent_flags):
            for hl in range(2):
                sl = slice(hl * LANES, (hl + 1) * LANES)
                chunks = [(lambda sl=sl: kc_ref[:, sl], lambda hl=hl: vc_aug[hl], None, None)]
                for t in range(seq // ROW_TILE if with_latent else 0):
                    rows = slice(t * ROW_TILE, (t + 1) * ROW_TILE)
                    chunks.append((lambda rows=rows, sl=sl: kl_ref[rows, sl],
                                   lambda hl=hl, t=t: vl_aug[hl, t], None, None))
                qs.append(q[:, sl])
                chunk_lists.append(chunks)
        accs = _flash_transposed(qs, chunk_lists, s_scr, p_scr, bounded)
        return [_normalise_pair(accs[2 * i], accs[2 * i + 1]) for i in range(len(q_tiles))]

    def body(t, carry):
        rows = [pl.ds(pl.multiple_of((TILES_PER_STEP * t + i) * ROW_TILE, ROW_TILE), ROW_TILE)
                for i in range(TILES_PER_STEP)]
        outs = tiles([ql_ref[r, :] for r in rows], [True] * TILES_PER_STEP)
        for r, o in zip(rows, outs):
            ol_ref[r, :] = o.astype(ol_ref.dtype)
        return carry

    n_loop = seq // (TILES_PER_STEP * ROW_TILE) - 1
    lax.fori_loop(0, n_loop, body, 0)
    rows = [slice((n_loop * TILES_PER_STEP + i) * ROW_TILE,
                  (n_loop * TILES_PER_STEP + i + 1) * ROW_TILE) for i in range(TILES_PER_STEP)]
    outs = tiles([ql_ref[r, :] for r in rows] + [qc_ref[...]], [True] * TILES_PER_STEP + [False])
    for r, o in zip(rows, outs):
        ol_ref[r, :] = o.astype(ol_ref.dtype)
    oc_ref[...] = outs[-1].astype(oc_ref.dtype)


def _attention_specs(n_batch, tiles_per_batch, q_lanes, batch_axis, max_keys, tiles_per_step):
    tq = ROW_TILE
    seq = tiles_per_batch * tq
    n_lat_tiles = n_batch * tiles_per_batch

    def lat(width):
        return pl.BlockSpec((seq, width), lambda *g: (g[batch_axis], g[1 - batch_axis]))

    def ctx(width):
        return pl.BlockSpec((tq, width),
                            lambda *g: (n_lat_tiles + g[batch_axis], g[1 - batch_axis]))

    in_specs = [lat(q_lanes), ctx(q_lanes), lat(q_lanes), ctx(q_lanes), lat(LANES), ctx(LANES)]
    out_specs = [lat(LANES),
                 pl.BlockSpec((tq, LANES), lambda *g: (g[batch_axis], g[1 - batch_axis]))]
    n_streams = 2 * (tiles_per_step + 1)
    scratch = [pltpu.VMEM((2, tiles_per_batch, LANES, tq), BF), pltpu.VMEM((2, LANES, tq), BF),
               pltpu.VMEM((n_streams, max_keys, tq), F32), pltpu.VMEM((n_streams, max_keys, tq), BF)]
    return in_specs, out_specs, scratch


def _mla_call(mq, mk, mv, n_batch, tiles_per_batch, bounded):
    seq = tiles_per_batch * ROW_TILE
    width = MLA_HEADS * MLA_V
    in_specs, out_specs, scratch = _attention_specs(
        n_batch, tiles_per_batch, 2 * LANES, 0, seq + ROW_TILE, TILES_PER_STEP)
    return pl.pallas_call(
        functools.partial(_mla_kernel, bounded=bounded),
        grid=(n_batch, MLA_HEADS // 2),
        in_specs=in_specs,
        out_specs=out_specs,
        out_shape=[jax.ShapeDtypeStruct((n_batch * seq, width), BF),
                   jax.ShapeDtypeStruct((n_batch * ROW_TILE, width), BF)],
        scratch_shapes=scratch,
        compiler_params=pltpu.CompilerParams(
            dimension_semantics=("arbitrary", "arbitrary"), vmem_limit_bytes=VMEM_LIMIT),
        name="mla_attention",
    )(mq, mq, mk, mk, mv, mv)


def _na_kernel(shift_ref, ql_ref, qc_ref, kl_ref, kc_ref, vl_ref, vc_ref, bias_ref, ol_ref, oc_ref,
               vl_aug, vc_aug, s_scr, p_scr, *, layer, bounded):
    n_lat_q = kl_ref.shape[0] // ROW_TILE
    lane = lax.broadcasted_iota(jnp.int32, (1, LANES), 1)
    shift = shift_ref[layer]
    _augment_values(vl_ref, vc_ref, vl_aug, vc_aug)

    def tiles(q_tiles, js):
        qs, chunk_lists = [], []
        for q, j in zip(q_tiles, js):
            if isinstance(j, int):
                first = min(max(j - 1, 0), n_lat_q - 3)
                pat = 0 if j == 0 else (2 if j == n_lat_q - 1 else 1)
            elif j is not None:
                first = jnp.clip(j - 1, 0, n_lat_q - 3)
                pat = jnp.where(j == 0, 0, jnp.where(j == n_lat_q - 1, 2, 1))
            for hl in range(2):
                in_head = (lane >= hl * HEAD_LANES) & (lane < (hl + 1) * HEAD_LANES)
                chunks = [(lambda: kc_ref[...], lambda hl=hl: vc_aug[hl], None, shift)]
                for c in range(NA_BAND // ROW_TILE if j is not None else 0):
                    rows = slice(c * ROW_TILE, (c + 1) * ROW_TILE)

                    def k_chunk(first=first, c=c):
                        start = (first + c) * ROW_TILE
                        if not isinstance(start, int):
                            start = pl.multiple_of(start, ROW_TILE)
                        return kl_ref[pl.ds(start, ROW_TILE), :]

                    chunks.append((k_chunk,
                                   lambda hl=hl, first=first, c=c: vl_aug[hl, first + c],
                                   lambda pat=pat, hl=hl, rows=rows: bias_ref[pat, hl, rows, :],
                                   None))
                qs.append(jnp.where(in_head, q, jnp.zeros_like(q)))
                chunk_lists.append(chunks)
        accs = _flash_transposed(qs, chunk_lists, s_scr, p_scr, bounded)
        return [_normalise_pair(accs[2 * i], accs[2 * i + 1]) for i in range(len(q_tiles))]

    def body(t, carry):
        js = [NA_TILES_PER_STEP * t + i for i in range(NA_TILES_PER_STEP)]
        rows = [pl.ds(pl.multiple_of(j * ROW_TILE, ROW_TILE), ROW_TILE) for j in js]
        outs = tiles([ql_ref[r, :] for r in rows], js)
        for r, o in zip(rows, outs):
            ol_ref[r, :] = o.astype(ol_ref.dtype)
        return carry

    n_loop = n_lat_q // NA_TILES_PER_STEP - 1
    lax.fori_loop(0, n_loop, body, 0)
    js = [n_loop * NA_TILES_PER_STEP + i for i in range(NA_TILES_PER_STEP)]
    rows = [slice(j * ROW_TILE, (j + 1) * ROW_TILE) for j in js]
    outs = tiles([ql_ref[r, :] for r in rows] + [qc_ref[...]], js + [None])
    for r, o in zip(rows, outs):
        ol_ref[r, :] = o.astype(ol_ref.dtype)
    oc_ref[...] = outs[-1].astype(oc_ref.dtype)


def _na_call(nq, nk, nv, bias, shifts, l, n_batch, tiles_per_batch, bounded):
    seq = tiles_per_batch * ROW_TILE
    width = NA_HEADS * NA_DIM
    in_specs, out_specs, scratch = _attention_specs(
        n_batch, tiles_per_batch, LANES, 1, NA_BAND + ROW_TILE, NA_TILES_PER_STEP)
    in_specs.insert(0, pl.BlockSpec(memory_space=pltpu.SMEM))
    in_specs.append(pl.BlockSpec((None, 3, 2, NA_BAND, ROW_TILE), lambda hp, b: (l, 0, hp, 0, 0)))
    return pl.pallas_call(
        functools.partial(_na_kernel, layer=l, bounded=bounded),
        grid=(NA_HEADS // 2, n_batch),
        in_specs=in_specs,
        out_specs=out_specs,
        out_shape=[jax.ShapeDtypeStruct((n_batch * seq, width), BF),
                   jax.ShapeDtypeStruct((n_batch * ROW_TILE, width), BF)],
        scratch_shapes=scratch,
        compiler_params=pltpu.CompilerParams(
            dimension_semantics=("arbitrary", "arbitrary"), vmem_limit_bytes=VMEM_LIMIT),
        name="na_attention",
    )(shifts, nq, nq, nk, nk, nv, nv, bias)


def _na_window_plan(rows):
    rows_per_tile = ROW_TILE // GRID_W
    n_tiles = rows // rows_per_tile
    band_rows = NA_BAND // GRID_W
    kh = min(NA_KH, rows)
    assert band_rows >= kh + rows_per_tile - 1 and rows_per_tile % 2 == 0
    plan = []
    for j in (0, 1, n_tiles - 1):
        band_start = int(np.clip(j - 1, 0, n_tiles - 3)) * rows_per_tile
        tile_plan = []
        for k_r in range(band_rows):
            key_row = band_start + k_r
            for qp in range(rows_per_tile // 2):
                ok = []
                for r in (j * rows_per_tile + 2 * qp, j * rows_per_tile + 2 * qp + 1):
                    r_start = int(np.clip(r - kh // 2, 0, rows - kh))
                    ok.append(r_start <= key_row < r_start + kh)
                d_left = key_row - (j * rows_per_tile + 2 * qp) + (NA_KH - 1)
                tile_plan.append((k_r, qp, int(np.clip(d_left, 0, 2 * NA_KH - 1)), ok[0], ok[1]))
        plan.append(tile_plan)
    return plan


def _na_bias_kernel(tab_ref, o_ref, *, plan):
    lane = lax.broadcasted_iota(jnp.int32, (1, 2 * GRID_W), 1)
    neg = jnp.full((GRID_W, 2 * GRID_W), NEG_BIG, F32)
    for pi, tile_plan in enumerate(plan):
        for k_r, qp, d_left, left_ok, right_ok in tile_plan:
            if left_ok and right_ok:
                blk = tab_ref[d_left]
            elif left_ok:
                blk = jnp.where(lane < GRID_W, tab_ref[d_left], NEG_BIG)
            elif right_ok:
                blk = jnp.where(lane < GRID_W, NEG_BIG, tab_ref[d_left])
            else:
                blk = neg
            o_ref[pi, k_r * GRID_W:(k_r + 1) * GRID_W, qp * 2 * GRID_W:(qp + 1) * 2 * GRID_W] = blk


def _na_bias_tables(rpb, rows, shifts):
    depth, heads = rpb.shape[:2]
    n_dr = 2 * NA_KH - 1
    k_c = np.arange(GRID_W)[:, None]
    q_c = np.arange(GRID_W)[None, :]
    c_start = np.clip(q_c - NA_KW // 2, 0, GRID_W - NA_KW)
    col_ok = (k_c >= c_start) & (k_c < c_start + NA_KW)
    dc = k_c - q_c + (NA_KW - 1)
    onehot = (np.arange(2 * NA_KW - 1)[:, None, None] == dc[None]).astype(np.float32)
    picked = jnp.einsum("lhdc,ckq->lhdkq", rpb, jnp.asarray(onehot),
                        precision=lax.Precision.HIGHEST)
    shifted = picked * LOG2E - shifts[:, None, None, None, None]
    col_tab = jnp.where(jnp.asarray(col_ok), shifted, NEG_BIG)
    neg = jnp.full((depth, heads, 1, GRID_W, GRID_W), NEG_BIG, F32)
    pair_tab = jnp.concatenate([jnp.concatenate([col_tab, neg], axis=2),
                                jnp.concatenate([neg, col_tab], axis=2)], axis=-1)
    plan = _na_window_plan(rows)
    return pl.pallas_call(
        functools.partial(_na_bias_kernel, plan=plan),
        grid=(depth, heads),
        in_specs=[pl.BlockSpec((None, None, n_dr + 1, GRID_W, 2 * GRID_W),
                               lambda l, h: (l, h, 0, 0, 0))],
        out_specs=pl.BlockSpec((None, 3, None, NA_BAND, ROW_TILE), lambda l, h: (l, 0, h, 0, 0)),
        out_shape=jax.ShapeDtypeStruct((depth, 3, heads, NA_BAND, ROW_TILE), F32),
        compiler_params=pltpu.CompilerParams(
            dimension_semantics=("arbitrary", "arbitrary"), vmem_limit_bytes=VMEM_LIMIT),
        name="na_bias_tables",
    )(pair_tab)


def _merge_mlp_kernel(xl_ref, xc_ref, yml_ref, ymc_ref, ynl_ref, ync_ref, gate_ref, gt1_ref, sh_ref,
                      sc_ref, gt2_ref, w_mo_ref, w_no_ref, w_out_ref, g_ref, w1_ref, w2_ref, o_ref,
                      *, n_lat_tiles):
    is_lat = pl.program_id(0) < n_lat_tiles
    x_in = jnp.where(is_lat, xl_ref[...], xc_ref[...])
    a = _dot(jnp.where(is_lat, yml_ref[...], ymc_ref[...]), w_mo_ref[...])
    b = _dot(jnp.where(is_lat, ynl_ref[...], ync_ref[...]), w_no_ref[...])
    y = gate_ref[:, :D_MODEL].astype(F32) * a + gate_ref[:, D_MODEL:].astype(F32) * b
    x = x_in + gt1_ref[...] * _dot(y.astype(BF), w_out_ref[...])

    h = (_rms(x, g_ref[...] * (1.0 + sc_ref[...])) + sh_ref[...]).astype(BF)
    d_ff = w1_ref.shape[1]
    chunk = 1024
    acc = None
    for c0 in range(0, d_ff, chunk):
        u = jnp.maximum(_dot(h, w1_ref[:, c0:c0 + chunk]), 0.0)
        part = _dot((u * u).astype(BF), w2_ref[c0:c0 + chunk, :])
        acc = part if acc is None else acc + part
    o_ref[...] = x + gt2_ref[...] * acc


def _merge_mlp_call(xs, y_mla, y_na, gate, mod, sw, l, n_tiles, n_lat_tiles, tiles_per_batch,
                    n_batch):
    tm = ROW_TILE
    mod_row = lambda i: jnp.where(i < n_lat_tiles, i // tiles_per_batch, n_batch)
    mod_spec = lambda col: pl.BlockSpec((None, None, 1, D_MODEL),
                                        lambda i: (l, mod_row(i), 0, col))
    row = lambda w: pl.BlockSpec((tm, w), lambda i: (i, 0))
    lat = lambda w: pl.BlockSpec((tm, w), lambda i: (jnp.minimum(i, n_lat_tiles - 1), 0))
    ctx = lambda w: pl.BlockSpec((tm, w), lambda i: (jnp.maximum(i - n_lat_tiles, 0), 0))
    names = ["w_mla_o", "w_na_o", "w_out", "g_mlp", "w_ff1", "w_ff2"]
    return pl.pallas_call(
        functools.partial(_merge_mlp_kernel, n_lat_tiles=n_lat_tiles),
        grid=(n_tiles,),
        in_specs=[*_x_specs(xs, n_lat_tiles), lat(y_mla[0].shape[1]), ctx(y_mla[1].shape[1]),
                  lat(y_na[0].shape[1]), ctx(y_na[1].shape[1]), row(gate.shape[1]),
                  mod_spec(2), mod_spec(3), mod_spec(4), mod_spec(5),
                  *[_layer(sw[n], l) for n in names]],
        out_specs=row(D_MODEL),
        out_shape=jax.ShapeDtypeStruct((n_tiles * tm, D_MODEL), F32),
        compiler_params=pltpu.CompilerParams(
            dimension_semantics=("arbitrary",), vmem_limit_bytes=VMEM_LIMIT),
        name="merge_mlp",
    )(xs[0], xs[1], y_mla[0], y_mla[1], y_na[0], y_na[1], gate, mod, mod, mod, mod,
      *[sw[n] for n in names])


def _take_cols(w, idx):
    idx = np.asarray(idx)
    pieces = []
    start = 0
    while start < len(idx):
        stop = start + 1
        if idx[start] < 0:
            while stop < len(idx) and idx[stop] < 0:
                stop += 1
            pieces.append(jnp.zeros(w.shape[:-1] + (stop - start,), w.dtype))
        else:
            while stop < len(idx) and idx[stop] == idx[stop - 1] + 1:
                stop += 1
            pieces.append(w[..., int(idx[start]):int(idx[stop - 1]) + 1])
        start = stop
    return jnp.concatenate(pieces, axis=-1)


def _pack_weights(w_in, w_uq, w_ukv, g_mla_q, g_mla_k, g_na_q, g_na_k, seq):
    local = _rope_swap_local()
    kr_local = local[MLA_NOPE:] - MLA_NOPE
    tail_idx = np.concatenate([
        np.full(MLA_NOPE, -1), kr_local,
        np.arange(MLA_ROPE, MLA_ROPE + 3 * NA_HEADS * NA_DIM)])
    uq_idx = np.concatenate([hd * MLA_QK + local for hd in range(MLA_HEADS)])
    per_head = MLA_NOPE + MLA_V
    ukv_k = np.concatenate([np.concatenate([hd * per_head + np.arange(MLA_NOPE),
                                            np.full(LANES - MLA_NOPE, -1)])
                            for hd in range(MLA_HEADS)])
    ukv_v = np.concatenate([hd * per_head + MLA_NOPE + np.arange(MLA_V) for hd in range(MLA_HEADS)])
    pair = np.concatenate([np.arange(NA_DIM), np.arange(NA_DIM)])

    cos_base, sin_base = _rope_tables(seq)
    partner = (np.arange(LANES) + MLA_ROPE) % LANES

    def rope_tabs(g, scale):
        g_ext = _take_cols(g, local)
        g_partner = _take_cols(g, local[partner])
        return (cos_base[None] * (g_ext * scale)[:, None, :],
                sin_base[None] * (g_partner * scale)[:, None, :])

    qcos, qsin = rope_tabs(g_mla_q, LOG2E / math.sqrt(MLA_QK))
    kcos, ksin = rope_tabs(g_mla_k, 1.0)
    return {
        "w_in_head": w_in.astype(BF),
        "w_in_tail": _take_cols(w_in[:, :, OFF_KR:].astype(BF), tail_idx),
        "w_uq": _take_cols(w_uq.astype(BF), uq_idx),
        "w_ukv": _take_cols(w_ukv.astype(BF), np.concatenate([ukv_k, ukv_v])),
        "qcos": qcos, "qsin": qsin, "kcos": kcos, "ksin": ksin,
        "g_nq": (_take_cols(g_na_q, pair) * (LOG2E / math.sqrt(NA_DIM)))[:, None, :],
        "g_nk": _take_cols(g_na_k, pair)[:, None, :],
    }


def _rope_tables(seq):
    t = np.arange(seq)
    inv_freq = jnp.asarray(ROPE_BASE, F32) ** (-jnp.arange(ROPE_PAIRS, dtype=F32) / ROPE_PAIRS)
    ang_r = jnp.asarray(t // GRID_W, F32)[:, None] * inv_freq
    ang_c = jnp.asarray(t % GRID_W, F32)[:, None] * inv_freq
    cr, sr, cc, sn = jnp.cos(ang_r), jnp.sin(ang_r), jnp.cos(ang_c), jnp.sin(ang_c)
    ones = jnp.ones((seq, MLA_NOPE), F32)
    zeros = jnp.zeros((seq, MLA_NOPE), F32)
    pad = jnp.zeros((seq, LANES - MLA_QK), F32)
    cos_lat = jnp.concatenate([ones, cr, cr, cc, cc, pad], axis=1)
    sin_lat = jnp.concatenate([zeros, -sr, sr, -sn, sn, pad], axis=1)
    cos_id = jnp.concatenate([jnp.ones((PROJ_TILE, MLA_QK), F32),
                              jnp.zeros((PROJ_TILE, LANES - MLA_QK), F32)], axis=1)
    sin_id = jnp.zeros((PROJ_TILE, LANES), F32)
    return jnp.concatenate([cos_lat, cos_id], axis=0), jnp.concatenate([sin_lat, sin_id], axis=0)


def kernel(x, c, ctx, c_ctx, w_ada, b_ada, g_attn, w_in, g_qa, w_uq, g_kva, w_ukv, g_mla_q,
           g_mla_k, g_na_q, g_na_k, rpb, w_mla_o, w_na_o, w_out, g_mlp, w_ff1, w_ff2):
    n_batch, seq, d = x.shape
    ctx_len = ctx.shape[1]
    depth = w_ada.shape[0]
    assert d == D_MODEL and ctx_len == ROW_TILE
    assert seq % PROJ_TILE == 0 and (n_batch * ctx_len) % PROJ_TILE == 0
    assert seq % (TILES_PER_STEP * ROW_TILE) == 0 and seq % (NA_TILES_PER_STEP * ROW_TILE) == 0
    tiles_per_batch = seq // ROW_TILE
    assert tiles_per_batch >= 3
    n_lat_tiles = n_batch * tiles_per_batch
    n_tiles = n_lat_tiles + n_batch
    rows_lat = n_batch * seq

    pad_rows = (-(n_batch + 1)) % 8
    cc = jnp.concatenate([c, c_ctx[None, :], jnp.zeros((pad_rows, d), F32)], axis=0)
    mod_all = _ada_call(cc, w_ada, b_ada)

    sw = _pack_weights(w_in, w_uq, w_ukv, g_mla_q, g_mla_k, g_na_q, g_na_k, seq)
    sw.update({
        "g_attn": g_attn[:, None, :], "g_qa": g_qa[:, None, :], "g_kva": g_kva[:, None, :],
        "g_mlp": g_mlp[:, None, :],
        "w_mla_o": w_mla_o.astype(BF), "w_na_o": w_na_o.astype(BF), "w_out": w_out.astype(BF),
        "w_ff1": w_ff1.astype(BF), "w_ff2": w_ff2.astype(BF),
    })
    gmax = lambda g: jnp.max(jnp.abs(g), axis=1)
    slack = 1.02
    mla_shift = slack * LOG2E * math.sqrt(MLA_QK) * gmax(g_mla_q) * gmax(g_mla_k)
    na_shift = (slack * LOG2E * math.sqrt(NA_DIM) * gmax(g_na_q) * gmax(g_na_k)
                + LOG2E * jnp.maximum(jnp.max(rpb, axis=(1, 2, 3)), 0.0))
    bounded_ok = jnp.maximum(jnp.max(mla_shift), jnp.max(na_shift)) <= SHIFT_LIMIT
    lane = np.arange(LANES)
    sw["q_pad"] = jnp.where(lane == MLA_QK, -mla_shift[:, None, None], 0.0).astype(F32)
    sw["k_pad"] = jnp.broadcast_to(jnp.asarray(lane == MLA_QK, F32), (depth, 1, LANES))

    bias_all = _na_bias_tables(rpb, seq // GRID_W, na_shift)
    mod = mod_all.reshape(depth, mod_all.shape[1], 1, 6 * d)
    def trunk(x_lat, x_ctx, bounded):
        xs = (x_lat, x_ctx, 0)
        for l in range(depth):
            gate, mq, mk, mv, nq, nk, nv = _proj_call(
                xs, n_tiles * ROW_TILE, rows_lat, seq, mod, sw, l, n_batch)
            y_mla = _mla_call(mq, mk, mv, n_batch, tiles_per_batch, bounded)
            y_na = _na_call(nq, nk, nv, bias_all, na_shift, l, n_batch, tiles_per_batch, bounded)
            upd_tiles = n_lat_tiles if l == depth - 1 else n_tiles
            x_all = _merge_mlp_call(xs, y_mla, y_na, gate, mod, sw, l, upd_tiles,
                                    n_lat_tiles, tiles_per_batch, n_batch)
            xs = (x_all, x_all, rows_lat)
        return x_all

    out = lax.cond(bounded_ok, functools.partial(trunk, bounded=True),
                   functools.partial(trunk, bounded=False),
                   x.reshape(rows_lat, d), ctx.reshape(n_batch * ctx_len, d))
    return out.reshape(n_batch, seq, d)
```
